```python
import jax, jax.numpy as jnp
from jax import lax
import numpy as np

D_MODEL = 1024
BATCH = 2
SEQ = 16384
DEPTH = 1
DEC_BATCH = 128
DEC_SEQ = 8
PAST_LEN = 8192
PAGE_SIZE = 128

CONV_CH = D_MODEL // 2
CONV_K = 31
FOX_W = D_MODEL // 2
FOX_HEAD_DIM = 64
FOX_HEADS = FOX_W // FOX_HEAD_DIM
IN_W = 2 * CONV_CH + 3 * FOX_W + FOX_HEADS
Q_BLOCK = 128
MEM_LEN = 256
XA_HEADS = 4
XA_HEAD_DIM = D_MODEL // XA_HEADS
PEER_HEADS = 8
N_KEYS = 128
N_EXPERTS = N_KEYS * N_KEYS
PEER_TOPK = 16
PEER_HALF = 128
PEER_BLOCK = 128
EPS = 1e-6
NEG_INF = -1e30

kernel_name = 'hymba_conformer_fox_peer_step'


def rms_norm(x, g):
    xf = x.astype(jnp.float32)
    y = xf * lax.rsqrt(jnp.mean(xf * xf, axis=-1, keepdims=True) + EPS)
    return (y * g.astype(jnp.float32)).astype(x.dtype)


def mix_in(xn, w_in, b_f):
    b, t, _ = xn.shape
    z = xn @ w_in
    o1 = CONV_CH
    o2 = 2 * CONV_CH
    o3 = o2 + FOX_W
    o4 = o3 + FOX_W
    o5 = o4 + FOX_W
    glu = z[..., :o1] * jax.nn.sigmoid(z[..., o1:o2])
    q = z[..., o2:o3].reshape(b, t, FOX_HEADS, FOX_HEAD_DIM)
    k = z[..., o3:o4].reshape(b, t, FOX_HEADS, FOX_HEAD_DIM)
    v = z[..., o4:o5].reshape(b, t, FOX_HEADS, FOX_HEAD_DIM)
    logf = jax.nn.log_sigmoid(z[..., o5:].astype(jnp.float32) + b_f.astype(jnp.float32))
    return glu, q, k, v, logf


def depthwise_conv_valid(u, w):
    return lax.conv_general_dilated(u, w[:, None, :].astype(u.dtype), (1,), 'VALID',
                                    dimension_numbers=('NWC', 'WIO', 'NWC'),
                                    feature_group_count=u.shape[-1])


def mix_out(conv, att, conv_b, ln_g, ln_b, w_out):
    y = conv.astype(jnp.float32) + conv_b.astype(jnp.float32)
    yc = y - jnp.mean(y, axis=-1, keepdims=True)
    y = yc * lax.rsqrt(jnp.mean(yc * yc, axis=-1, keepdims=True) + EPS) * ln_g.astype(jnp.float32) + ln_b.astype(jnp.float32)
    y = jax.nn.silu(y).astype(att.dtype)
    return jnp.concatenate([y, att], axis=-1) @ w_out


def fox_prompt(q, k, v, logf):
    b, s, h, dh = q.shape
    nb = s // Q_BLOCK
    c = jnp.cumsum(logf, axis=1)
    c_keys = c.transpose(0, 2, 1)
    kf = k.astype(jnp.float32)
    vf = v.astype(jnp.float32)
    qb = (q.astype(jnp.float32) * dh ** -0.5).reshape(b, nb, Q_BLOCK, h, dh).transpose(1, 0, 2, 3, 4)
    cb = c_keys.reshape(b, h, nb, Q_BLOCK).transpose(2, 0, 1, 3)
    kpos = jnp.arange(s)

    def block(args):
        qi, ci, i = args
        logits = jnp.einsum('bqhd,bkhd->bhqk', qi, kf) + ci[..., None] - c_keys[:, :, None, :]
        qpos = i * Q_BLOCK + jnp.arange(Q_BLOCK)
        logits = jnp.where(kpos[None, :] <= qpos[:, None], logits, NEG_INF)
        p = jax.nn.softmax(logits, axis=-1)
        return jnp.einsum('bhqk,bkhd->bqhd', p, vf)

    out = lax.map(block, (qb, cb, jnp.arange(nb)))
    return out.transpose(1, 0, 2, 3, 4).reshape(b, s, h * dh).astype(q.dtype)


def fox_sample(q, k, v, logf, cache_k, cache_v, cache_logf, layer, page_table):
    b, t, h, dh = q.shape
    n_pages = page_table.shape[1]
    f32 = jnp.float32
    qs = q.astype(f32) * dh ** -0.5
    c_new = jnp.cumsum(logf, axis=1).transpose(0, 2, 1)
    past_logf = cache_logf[layer, page_table].astype(f32).reshape(b, n_pages * PAGE_SIZE, h)
    suffix = lax.cumsum(past_logf, axis=1, reverse=True) - past_logf
    suffix = suffix.reshape(b, n_pages, PAGE_SIZE, h).transpose(1, 0, 3, 2)

    def page_step(carry, inp):
        m, den, acc = carry
        pages, suf = inp
        kp = cache_k[layer, pages].astype(f32)
        vp = cache_v[layer, pages].astype(f32)
        logits = jnp.einsum('bthd,bshd->bhts', qs, kp) + suf[:, :, None, :] + c_new[..., None]
        m_new = jnp.maximum(m, logits.max(axis=-1))
        p = jnp.exp(logits - m_new[..., None])
        corr = jnp.exp(m - m_new)
        return (m_new, den * corr + p.sum(axis=-1),
                acc * corr[..., None] + jnp.einsum('bhts,bshd->bhtd', p, vp)), None

    init = (jnp.full((b, h, t), NEG_INF, f32), jnp.zeros((b, h, t), f32), jnp.zeros((b, h, t, dh), f32))
    (m, den, acc), _ = lax.scan(page_step, init, (page_table.T, suffix))
    logits = jnp.einsum('bthd,bshd->bhts', qs, k.astype(f32)) + c_new[..., :, None] - c_new[..., None, :]
    logits = jnp.where(jnp.tril(jnp.ones((t, t), bool)), logits, NEG_INF)
    m_new = jnp.maximum(m, logits.max(axis=-1))
    p = jnp.exp(logits - m_new[..., None])
    corr = jnp.exp(m - m_new)
    den = den * corr + p.sum(axis=-1)
    acc = acc * corr[..., None] + jnp.einsum('bhts,bshd->bhtd', p, v.astype(f32))
    out = acc / den[..., None]
    return out.transpose(0, 2, 1, 3).reshape(b, t, h * dh).astype(q.dtype)


def memory_kv(mem, g, wk, wv):
    b, n, _ = mem.shape
    mn = rms_norm(mem, g)
    return ((mn @ wk).reshape(b, n, XA_HEADS, XA_HEAD_DIM),
            (mn @ wv).reshape(b, n, XA_HEADS, XA_HEAD_DIM))


def cross_attn(hn, mk, mv, wq, wo):
    b, t, _ = hn.shape
    q = (hn @ wq).reshape(b, t, XA_HEADS, XA_HEAD_DIM).astype(jnp.float32) * XA_HEAD_DIM ** -0.5
    p = jax.nn.softmax(jnp.einsum('bthd,bmhd->bhtm', q, mk.astype(jnp.float32)), axis=-1)
    o = jnp.einsum('bhtm,bmhd->bthd', p, mv.astype(jnp.float32)).reshape(b, t, D_MODEL).astype(hn.dtype)
    return o @ wo


def peer(hn, w_query, sub_keys, expert_u, expert_v):
    b, t, d = hn.shape
    n = b * t
    pad = (-n) % PEER_BLOCK
    xb = jnp.pad(hn.reshape(n, d), ((0, pad), (0, 0))).reshape(-1, PEER_BLOCK, d)
    sk = sub_keys.astype(jnp.float32)

    def block(xblk):
        q = (xblk @ w_query).astype(jnp.float32).reshape(PEER_BLOCK, PEER_HEADS, 2, PEER_HALF)
        s = jnp.einsum('nhcd,hckd->nhck', q, sk)
        top_s, top_i = lax.top_k(s, PEER_TOPK)
        cand_s = (top_s[:, :, 0, :, None] + top_s[:, :, 1, None, :]).reshape(PEER_BLOCK, PEER_HEADS, PEER_TOPK * PEER_TOPK)
        cand_i = (top_i[:, :, 0, :, None] * N_KEYS + top_i[:, :, 1, None, :]).reshape(PEER_BLOCK, PEER_HEADS, PEER_TOPK * PEER_TOPK)
        best_s, best_j = lax.top_k(cand_s, PEER_TOPK)
        idx = jnp.take_along_axis(cand_i, best_j, axis=-1)
        g = jax.nn.softmax(best_s, axis=-1)
        act = jax.nn.gelu(jnp.einsum('nd,nhkd->nhk', xblk, expert_u[idx]).astype(jnp.float32), approximate=False)
        return jnp.einsum('nhk,nhkd->nd', (g * act).astype(xblk.dtype), expert_v[idx])

    out = lax.map(block, xb).reshape(-1, d)[:n]
    return out.reshape(b, t, d)


def setup_inputs(seed: int = 0) -> dict:
    key = jax.random.key(seed)
    ks = jax.random.split(key, 32)
    f32 = jnp.float32
    n_pages = PAST_LEN // PAGE_SIZE
    n_used = DEC_BATCH * n_pages
    n_pool = n_used + max(1, n_used // 4)
    sd = D_MODEL ** -0.5

    def nrm(k, shape, s):
        return jax.random.normal(k, shape, f32) * s

    def gain(k, shape):
        return 1.0 + 0.02 * jax.random.normal(k, shape, f32)

    page_table = jax.random.permutation(ks[0], n_pool)[:n_used].reshape(DEC_BATCH, n_pages).astype(jnp.int32)
    cache_logf = jax.nn.log_sigmoid(
        jax.random.uniform(ks[1], (DEPTH, n_pool, PAGE_SIZE, FOX_HEADS), f32, 2.0, 6.0)
        + nrm(ks[2], (DEPTH, n_pool, PAGE_SIZE, FOX_HEADS), 1.0))
    return {
        'x_prompt': nrm(ks[3], (BATCH, SEQ, D_MODEL), 1.0),
        'x_sample': nrm(ks[4], (DEC_BATCH, DEC_SEQ, D_MODEL), 1.0),
        'mem_prompt': nrm(ks[5], (BATCH, MEM_LEN, D_MODEL), 1.0),
        'cache_k': nrm(ks[6], (DEPTH, n_pool, PAGE_SIZE, FOX_HEADS, FOX_HEAD_DIM), 1.0),
        'cache_v': nrm(ks[7], (DEPTH, n_pool, PAGE_SIZE, FOX_HEADS, FOX_HEAD_DIM), 1.0),
        'cache_logf': cache_logf,
        'page_table': page_table,
        'state_conv': nrm(ks[8], (DEPTH, DEC_BATCH, CONV_K - 1, CONV_CH), 0.5),
        'cache_mem_k': nrm(ks[9], (DEPTH, DEC_BATCH, MEM_LEN, XA_HEADS, XA_HEAD_DIM), 1.0),
        'cache_mem_v': nrm(ks[10], (DEPTH, DEC_BATCH, MEM_LEN, XA_HEADS, XA_HEAD_DIM), 1.0),
        'norm1_g': gain(ks[11], (DEPTH, D_MODEL)),
        'w_in': nrm(ks[12], (DEPTH, D_MODEL, IN_W), sd),
        'b_f': jax.random.uniform(ks[13], (DEPTH, FOX_HEADS), f32, 2.0, 6.0),
        'conv_w': nrm(ks[14], (DEPTH, CONV_K, CONV_CH), CONV_K ** -0.5),
        'conv_b': nrm(ks[15], (DEPTH, CONV_CH), 0.02),
        'conv_ln_g': gain(ks[16], (DEPTH, CONV_CH)),
        'conv_ln_b': nrm(ks[17], (DEPTH, CONV_CH), 0.02),
        'w_out': nrm(ks[18], (DEPTH, CONV_CH + FOX_W, D_MODEL), (CONV_CH + FOX_W) ** -0.5),
        'norm2_g': gain(ks[19], (DEPTH, D_MODEL)),
        'mem_norm_g': gain(ks[20], (DEPTH, D_MODEL)),
        'wq_x': nrm(ks[21], (DEPTH, D_MODEL, D_MODEL), sd),
        'wk_x': nrm(ks[22], (DEPTH, D_MODEL, D_MODEL), sd),
        'wv_x': nrm(ks[23], (DEPTH, D_MODEL, D_MODEL), sd),
        'wo_x': nrm(ks[24], (DEPTH, D_MODEL, D_MODEL), sd),
        'norm3_g': gain(ks[25], (DEPTH, D_MODEL)),
        'w_query': nrm(ks[26], (DEPTH, D_MODEL, PEER_HEADS * 2 * PEER_HALF), sd),
        'sub_keys': nrm(ks[27], (DEPTH, PEER_HEADS, 2, N_KEYS, PEER_HALF), PEER_HALF ** -0.5),
        'expert_u': nrm(ks[28], (DEPTH, N_EXPERTS, D_MODEL), sd),
        'expert_v': nrm(ks[29], (DEPTH, N_EXPERTS, D_MODEL), PEER_HEADS ** -0.5),
        'normf_g': gain(ks[30], (D_MODEL,)),
    }


def reference(x_prompt, x_sample, mem_prompt, cache_k, cache_v, cache_logf, page_table, state_conv,
              cache_mem_k, cache_mem_v, norm1_g, w_in, b_f, conv_w, conv_b, conv_ln_g, conv_ln_b, w_out,
              norm2_g, mem_norm_g, wq_x, wk_x, wv_x, wo_x, norm3_g, w_query, sub_keys, expert_u,
              expert_v, normf_g):
    yp = x_prompt
    ys = x_sample
    kp_l, vp_l, fp_l, cp_l, mkp_l, mvp_l = [], [], [], [], [], []
    ks_l, vs_l, fs_l, cs_l = [], [], [], []
    for l in range(DEPTH):
        glu, q, k, v, logf = mix_in(rms_norm(yp, norm1_g[l]), w_in[l], b_f[l])
        conv = depthwise_conv_valid(jnp.pad(glu, ((0, 0), (CONV_K - 1, 0), (0, 0))), conv_w[l])
        att = fox_prompt(q, k, v, logf)
        yp = yp + mix_out(conv, att, conv_b[l], conv_ln_g[l], conv_ln_b[l], w_out[l])
        mk, mv = memory_kv(mem_prompt, mem_norm_g[l], wk_x[l], wv_x[l])
        yp = yp + cross_attn(rms_norm(yp, norm2_g[l]), mk, mv, wq_x[l], wo_x[l])
        yp = yp + peer(rms_norm(yp, norm3_g[l]), w_query[l], sub_keys[l], expert_u[l], expert_v[l])
        kp_l.append(k)
        vp_l.append(v)
        fp_l.append(logf)
        cp_l.append(glu[:, -(CONV_K - 1):])
        mkp_l.append(mk)
        mvp_l.append(mv)
        glu, q, k, v, logf = mix_in(rms_norm(ys, norm1_g[l]), w_in[l], b_f[l])
        conv_in = jnp.concatenate([state_conv[l].astype(glu.dtype), glu], axis=1)
        conv = depthwise_conv_valid(conv_in, conv_w[l])
        att = fox_sample(q, k, v, logf, cache_k, cache_v, cache_logf, l, page_table)
        ys = ys + mix_out(conv, att, conv_b[l], conv_ln_g[l], conv_ln_b[l], w_out[l])
        ys = ys + cross_attn(rms_norm(ys, norm2_g[l]), cache_mem_k[l], cache_mem_v[l], wq_x[l], wo_x[l])
        ys = ys + peer(rms_norm(ys, norm3_g[l]), w_query[l], sub_keys[l], expert_u[l], expert_v[l])
        ks_l.append(k)
        vs_l.append(v)
        fs_l.append(logf)
        cs_l.append(conv_in[:, -(CONV_K - 1):])
    return (rms_norm(yp, normf_g), rms_norm(ys, normf_g),
            jnp.stack(kp_l), jnp.stack(vp_l), jnp.stack(fp_l), jnp.stack(cp_l),
            jnp.stack(mkp_l), jnp.stack(mvp_l),
            jnp.stack(ks_l), jnp.stack(vs_l), jnp.stack(fs_l), jnp.stack(cs_l))
```

```python
import functools

import jax
import jax.numpy as jnp
from jax import lax
from jax.experimental import pallas as pl
from jax.experimental.pallas import tpu as pltpu

F32 = jnp.float32
BF16 = jnp.bfloat16
I32 = jnp.int32

EPS = 1e-6
NEG_INF = -1e30
LANES = 128
SUBLANES = 8
VMEM_LIMIT = 48 * 1024 * 1024
SAMPLE_GROUPS = 4


def _params(*sem):
    return pltpu.CompilerParams(dimension_semantics=sem, vmem_limit_bytes=VMEM_LIMIT)


def _dot(a, b):
    return jnp.dot(a, b, preferred_element_type=F32)


def _dot_nt(a, b):
    return lax.dot_general(a, b, (((1,), (1,)), ((), ())), preferred_element_type=F32)


def _rms(x, g):
    return x * lax.rsqrt(jnp.mean(x * x, axis=-1, keepdims=True) + EPS) * g


def _split3(x):
    hi = x.astype(BF16)
    r = x - hi.astype(F32)
    mid = r.astype(BF16)
    lo = (r - mid.astype(F32)).astype(BF16)
    return hi, mid, lo


def _inproj_kernel(x_ref, g_ref, w_ref, wf_ref, bf_ref,
                   glu_ref, q_ref, k_ref, v_ref, kb_ref, vb_ref, lf_ref, *, cc, fw, heads, qscale):
    xb = _rms(x_ref[...], g_ref[...]).astype(BF16)

    def mm(lo, n):
        return _dot(xb, w_ref[:, lo:lo + n])

    glu_ref[...] = mm(0, cc) * jax.nn.sigmoid(mm(cc, cc))
    q_ref[...] = (mm(2 * cc, fw) * qscale).astype(BF16)
    k = mm(2 * cc + fw, fw)
    k_ref[...] = k
    kb_ref[...] = k.astype(BF16)
    v = mm(2 * cc + 2 * fw, fw)
    v_ref[...] = v
    vb_ref[...] = v.astype(BF16)
    z = _dot(xb, wf_ref[...])[:, :heads] + bf_ref[...]
    lf_ref[...] = jnp.minimum(z, 0.0) - jnp.log1p(jnp.exp(-jnp.abs(z)))


def _inproj(x, g, w_main, w_f, b_f, *, cc, fw, heads, head_dim, tile=512):
    n, d = x.shape
    tile = min(tile, n)
    row = lambda w: pl.BlockSpec((tile, w), lambda i: (i, 0))
    full = lambda a: pl.BlockSpec(a.shape, lambda i: (0,) * a.ndim)
    out_shape = (jax.ShapeDtypeStruct((n, cc), F32), jax.ShapeDtypeStruct((n, fw), BF16),
                 jax.ShapeDtypeStruct((n, fw), F32), jax.ShapeDtypeStruct((n, fw), F32),
                 jax.ShapeDtypeStruct((n, fw), BF16), jax.ShapeDtypeStruct((n, fw), BF16),
                 jax.ShapeDtypeStruct((n, heads), F32))
    return pl.pallas_call(
        functools.partial(_inproj_kernel, cc=cc, fw=fw, heads=heads, qscale=head_dim ** -0.5),
        grid=(n // tile,),
        in_specs=[row(d), full(g), full(w_main), full(w_f), full(b_f)],
        out_specs=(row(cc), row(fw), row(fw), row(fw), row(fw), row(fw), row(heads)),
        out_shape=out_shape, compiler_params=_params("parallel"), name="inproj",
    )(x, g, w_main, w_f, b_f)


def _cumsum_kernel(x_ref, o_ref, *, nchunk):
    h = x_ref.shape[1]
    lane = lax.broadcasted_iota(I32, (h, LANES), 1)

    def body(i, carry):
        off = pl.multiple_of(i * LANES, LANES)
        y = x_ref[0, :, pl.ds(off, LANES)]
        d = 1
        while d < LANES:
            y = y + jnp.where(lane >= d, pltpu.roll(y, d, axis=1), 0.0)
            d *= 2
        y = y + carry
        o_ref[0, :, pl.ds(off, LANES)] = y
        return y[:, LANES - 1:LANES]

    lax.fori_loop(0, nchunk, body, jnp.zeros((h, 1), F32))


def _cumsum_lanes(x):
    b, h, s = x.shape
    spec = pl.BlockSpec((1, h, s), lambda i: (i, 0, 0))
    return pl.pallas_call(
        functools.partial(_cumsum_kernel, nchunk=s // LANES), grid=(b,),
        in_specs=[spec], out_specs=spec, out_shape=jax.ShapeDtypeStruct(x.shape, F32),
        compiler_params=_params("parallel"), name="cumsum",
    )(x)


def _fox_prompt_kernel(q_ref, k_ref, v_ref, c_ref, o_ref, m_ref, l_ref, acc_ref, *, tq, hd):
    i = pl.program_id(2)
    q = q_ref[0]
    lane = lax.broadcasted_iota(I32, (tq, 2 * hd), 1)
    first = lane < hd
    zero = jnp.zeros_like(q)
    qh = (jnp.where(first, q, zero), jnp.where(first, zero, q))
    last = pl.multiple_of((i + 1) * tq - LANES, LANES)
    c_last = c_ref[0, 0, :, pl.ds(last, LANES)][:, LANES - 1:LANES]

    m_ref[...] = jnp.full(m_ref.shape, NEG_INF, F32)
    l_ref[...] = jnp.zeros(l_ref.shape, F32)
    acc_ref[...] = jnp.zeros(acc_ref.shape, F32)

    def step(j, masked):
        off = pl.multiple_of(j * tq, tq)
        kb = k_ref[0, pl.ds(off, tq), :]
        vb = v_ref[0, pl.ds(off, tq), :]
        bias = c_last - c_ref[0, 0, :, pl.ds(off, tq)]
        pv = []
        alphas = []
        for h in range(2):
            s = _dot_nt(qh[h], kb) + bias[h:h + 1, :]
            if masked:
                r = lax.broadcasted_iota(I32, (tq, tq), 0)
                c = lax.broadcasted_iota(I32, (tq, tq), 1)
                s = jnp.where(c <= r, s, NEG_INF)
            m_old = m_ref[h]
            m_new = jnp.maximum(m_old, jnp.max(s, axis=-1, keepdims=True))
            p = jnp.exp(s - m_new)
            alpha = jnp.exp(m_old - m_new)
            l_ref[h] = alpha * l_ref[h] + jnp.sum(p, axis=-1, keepdims=True)
            m_ref[h] = m_new
            pv.append(_dot(p.astype(BF16), vb))
            alphas.append(alpha)
        acc_ref[...] = (acc_ref[...] * jnp.where(first, alphas[0], alphas[1])
                        + jnp.where(first, pv[0], pv[1]))

    def body(j, carry):
        step(j, False)
        return carry

    lax.fori_loop(0, i, body, 0)
    step(i, True)
    inv = jnp.where(first, 1.0 / l_ref[0], 1.0 / l_ref[1])
    o_ref[0] = (acc_ref[...] * inv).astype(o_ref.dtype)


def _fox_prompt(qb, kb, vb, c_pairs, *, hd, tq=512):
    b, s, w = qb.shape
    tq = min(tq, s)
    pairs = w // (2 * hd)
    qspec = pl.BlockSpec((1, tq, 2 * hd), lambda bi, p, i: (bi, i, p))
    kspec = pl.BlockSpec((1, s, 2 * hd), lambda bi, p, i: (bi, 0, p))
    cspec = pl.BlockSpec((1, 1, 2, s), lambda bi, p, i: (bi, p, 0, 0))
    return pl.pallas_call(
        functools.partial(_fox_prompt_kernel, tq=tq, hd=hd),
        grid=(b, pairs, s // tq),
        in_specs=[qspec, kspec, kspec, cspec], out_specs=qspec,
        out_shape=jax.ShapeDtypeStruct((b, s, w), BF16),
        scratch_shapes=[pltpu.VMEM((2, tq, 1), F32), pltpu.VMEM((2, tq, 1), F32),
                        pltpu.VMEM((tq, 2 * hd), F32)],
        compiler_params=_params("parallel", "parallel", "arbitrary"), name="fox_prompt",
    )(qb, kb, vb, c_pairs)


def _fox_sample_kernel(pt_ref, q_ref, kn_ref, vn_ref, lfn_ref, ck_hbm, cv_hbm, clf_hbm, o_ref,
                       kbuf, vbuf, lfbuf, knew, vnew, lfnew, m_ref, l_ref, acc_ref, pre_ref, sem,
                       *, nb, t, heads, hd, pg, page, ngroups):
    rows = heads * t
    width = heads * hd

    def copies(step, slot):
        b = step // ngroups
        g = step % ngroups
        out = []
        for i in range(pg):
            pid = pt_ref[b, g * pg + i]
            dst = pl.ds(i * page, page)
            out.append(pltpu.make_async_copy(ck_hbm.at[pid], kbuf.at[slot, dst], sem.at[slot, 0]))
            out.append(pltpu.make_async_copy(cv_hbm.at[pid], vbuf.at[slot, dst], sem.at[slot, 1]))
            out.append(pltpu.make_async_copy(clf_hbm.at[pid], lfbuf.at[slot, dst], sem.at[slot, 2]))
        return out

    knew[...] = jnp.zeros(knew.shape, F32)
    vnew[...] = jnp.zeros(vnew.shape, F32)
    lfnew[...] = jnp.zeros(lfnew.shape, F32)

    r_i = lax.broadcasted_iota(I32, (rows, width), 0)
    c_i = lax.broadcasted_iota(I32, (rows, width), 1)
    blockdiag = (r_i // t) == (c_i // hd)
    e_r = lax.broadcasted_iota(I32, (rows, heads), 0)
    e_c = lax.broadcasted_iota(I32, (rows, heads), 1)
    expand_t = jnp.where((e_r // t) == e_c, 1.0, 0.0).astype(BF16)
    u_r = lax.broadcasted_iota(I32, (page, page), 0)
    u_c = lax.broadcasted_iota(I32, (page, page), 1)
    tri = jnp.where(u_r <= u_c, 1.0, 0.0).astype(BF16)
    n_r = lax.broadcasted_iota(I32, (rows, page), 0)
    n_c = lax.broadcasted_iota(I32, (rows, page), 1)
    new_mask = n_c <= (n_r % t)

    def page_update(qbd, kp, vp, lf, mask):
        pre = jnp.zeros((rows, page), F32)
        for piece in _split3(lf):
            lfx = _dot_nt(expand_t, piece)
            pre = pre + _dot(lfx.astype(BF16), tri)
        tot = pre_ref[...] + pre
        pre_ref[...] = tot[:, page - 1:page]
        s = _dot_nt(qbd, kp.astype(BF16)) - tot
        if mask is not None:
            s = jnp.where(mask, s, NEG_INF)
        m_old = m_ref[...]
        m_new = jnp.maximum(m_old, jnp.max(s, axis=-1, keepdims=True))
        p = jnp.exp(s - m_new)
        alpha = jnp.exp(m_old - m_new)
        l_ref[...] = alpha * l_ref[...] + jnp.sum(p, axis=-1, keepdims=True)
        m_ref[...] = m_new
        acc_ref[...] = alpha * acc_ref[...] + _dot(p.astype(BF16), vp.astype(BF16))

    for c in copies(0, 0):
        c.start()

    def body(step, carry):
        slot = step % 2
        b = step // ngroups
        g = step % ngroups
        row0 = pl.multiple_of(b * t, t)

        @pl.when(step + 1 < nb * ngroups)
        def _():
            for c in copies(step + 1, 1 - slot):
                c.start()

        @pl.when(g == 0)
        def _():
            m_ref[...] = jnp.full(m_ref.shape, NEG_INF, F32)
            l_ref[...] = jnp.zeros(l_ref.shape, F32)
            acc_ref[...] = jnp.zeros(acc_ref.shape, F32)
            pre_ref[...] = jnp.zeros(pre_ref.shape, F32)

        q8 = q_ref[pl.ds(row0, t), :]
        qbd = jnp.where(blockdiag, jnp.concatenate([q8] * heads, axis=0), 0.0).astype(BF16)

        for c in copies(step, slot):
            c.wait()
        for i in range(pg):
            sl = pl.ds(i * page, page)
            page_update(qbd, kbuf[slot, sl, :], vbuf[slot, sl, :], lfbuf[slot, sl, :], None)

        @pl.when(g == ngroups - 1)
        def _():
            knew[0:t, :] = kn_ref[pl.ds(row0, t), :]
            vnew[0:t, :] = vn_ref[pl.ds(row0, t), :]
            lfnew[0:t, :] = lfn_ref[pl.ds(row0, t), :]
            page_update(qbd, knew[...], vnew[...], lfnew[...], new_mask)
            o = jnp.where(blockdiag, acc_ref[...] / l_ref[...], 0.0)
            out = o[0:t]
            for h in range(1, heads):
                out = out + o[h * t:(h + 1) * t]
            o_ref[pl.ds(row0, t), :] = out

        return carry

    lax.fori_loop(0, nb * ngroups, body, 0)


def _fox_sample(page_table, q, k_new, v_new, lf_new, cache_k, cache_v, cache_lf, *, t, heads, hd, pg=8):
    nb, n_pages = page_table.shape
    page = cache_k.shape[1]
    width = heads * hd
    rows = heads * t
    pg = min(pg, n_pages)
    ngroups = n_pages // pg
    vm = lambda a: pl.BlockSpec(a.shape, lambda i, pt: (0,) * a.ndim)
    anyspec = pl.BlockSpec(memory_space=pl.ANY)
    grid_spec = pltpu.PrefetchScalarGridSpec(
        num_scalar_prefetch=1, grid=(1,),
        in_specs=[vm(q), vm(k_new), vm(v_new), vm(lf_new), anyspec, anyspec, anyspec],
        out_specs=pl.BlockSpec((nb * t, width), lambda i, pt: (0, 0)),
        scratch_shapes=[pltpu.VMEM((2, pg * page, width), F32), pltpu.VMEM((2, pg * page, width), F32),
                        pltpu.VMEM((2, pg * page, heads), F32),
                        pltpu.VMEM((page, width), F32), pltpu.VMEM((page, width), F32),
                        pltpu.VMEM((page, heads), F32),
                        pltpu.VMEM((rows, 1), F32), pltpu.VMEM((rows, 1), F32),
                        pltpu.VMEM((rows, width), F32), pltpu.VMEM((rows, 1), F32),
                        pltpu.SemaphoreType.DMA((2, 3))])
    return pl.pallas_call(
        functools.partial(_fox_sample_kernel, nb=nb, t=t, heads=heads, hd=hd, pg=pg, page=page,
                          ngroups=ngroups),
        grid_spec=grid_spec, out_shape=jax.ShapeDtypeStruct((nb * t, width), F32),
        compiler_params=_params("arbitrary"), name="fox_sample",
    )(page_table, q, k_new, v_new, lf_new, cache_k, cache_v, cache_lf)


def _ln_silu(y, conv_b, ln_g, ln_b):
    y = y + conv_b
    yc = y - jnp.mean(y, axis=-1, keepdims=True)
    y = yc * lax.rsqrt(jnp.mean(yc * yc, axis=-1, keepdims=True) + EPS) * ln_g + ln_b
    return y * jax.nn.sigmoid(y)


def _conv_prompt_kernel(cur_ref, halo_ref, w_ref, cb_ref, lg_ref, lb_ref, o_ref, ext, *, tile, halo, taps, chunk):
    i = pl.program_id(1)
    ext[0:halo, :] = jnp.where(i > 0, halo_ref[0], 0.0)
    ext[halo:halo + tile, :] = cur_ref[0]
    base = halo - (taps - 1)
    for r0 in range(0, tile, chunk):
        acc = jnp.zeros((chunk, ext.shape[1]), F32)
        for j in range(taps):
            acc = acc + w_ref[j:j + 1, :] * ext[r0 + base + j:r0 + base + j + chunk, :]
        o_ref[0, r0:r0 + chunk, :] = _ln_silu(acc, cb_ref[...], lg_ref[...], lb_ref[...])


def _conv_prompt(glu, conv_w, conv_b, ln_g, ln_b, *, tile=512, halo=32, chunk=64):
    b, s, c = glu.shape
    taps = conv_w.shape[0]
    tile = min(tile, s)
    per = tile // halo
    cur = pl.BlockSpec((1, tile, c), lambda bi, i: (bi, i, 0))
    hal = pl.BlockSpec((1, halo, c), lambda bi, i: (bi, jnp.maximum(i * per - 1, 0), 0))
    full = lambda a: pl.BlockSpec(a.shape, lambda bi, i: (0,) * a.ndim)
    return pl.pallas_call(
        functools.partial(_conv_prompt_kernel, tile=tile, halo=halo, taps=taps, chunk=chunk),
        grid=(b, s // tile),
        in_specs=[cur, hal, full(conv_w), full(conv_b), full(ln_g), full(ln_b)],
        out_specs=cur, out_shape=jax.ShapeDtypeStruct((b, s, c), F32),
        scratch_shapes=[pltpu.VMEM((halo + tile, c), F32)],
        compiler_params=_params("parallel", "parallel"), name="conv_prompt",
    )(glu, glu, conv_w, conv_b, ln_g, ln_b)


def _conv_sample_kernel(st_ref, glu_ref, w_ref, cb_ref, lg_ref, lb_ref, o_ref, ns_ref, ext, *, bt, t, taps):
    hist = taps - 1
    for b in range(bt):
        ext[0:hist, :] = st_ref[b]
        ext[hist:hist + t, :] = glu_ref[b]
        acc = jnp.zeros((t, ext.shape[1]), F32)
        for j in range(taps):
            acc = acc + w_ref[j:j + 1, :] * ext[j:j + t, :]
        o_ref[b] = _ln_silu(acc, cb_ref[...], lg_ref[...], lb_ref[...])
        ns_ref[b] = ext[t:t + hist, :]


def _conv_sample(state, glu, conv_w, conv_b, ln_g, ln_b, *, bt=8):
    nb, hist, c = state.shape
    t = glu.shape[1]
    taps = conv_w.shape[0]
    bt = min(bt, nb)
    st = pl.BlockSpec((bt, hist, c), lambda i: (i, 0, 0))
    gl = pl.BlockSpec((bt, t, c), lambda i: (i, 0, 0))
    full = lambda a: pl.BlockSpec(a.shape, lambda i: (0,) * a.ndim)
    return pl.pallas_call(
        functools.partial(_conv_sample_kernel, bt=bt, t=t, taps=taps), grid=(nb // bt,),
        in_specs=[st, gl, full(conv_w), full(conv_b), full(ln_g), full(ln_b)],
        out_specs=(gl, st),
        out_shape=(jax.ShapeDtypeStruct((nb, t, c), F32), jax.ShapeDtypeStruct((nb, hist, c), F32)),
        scratch_shapes=[pltpu.VMEM((hist + t + 2, c), F32)],
        compiler_params=_params("parallel"), name="conv_sample",
    )(state, glu, conv_w, conv_b, ln_g, ln_b)


def _memkv_kernel(m_ref, g_ref, wk_ref, wv_ref, k_ref, v_ref):
    mb = _rms(m_ref[...], g_ref[...]).astype(BF16)
    k_ref[...] = _dot(mb, wk_ref[...])
    v_ref[...] = _dot(mb, wv_ref[...])


def _memkv(mem, g, wk, wv, *, tile=256):
    n, d = mem.shape
    tile = min(tile, n)
    row = pl.BlockSpec((tile, d), lambda i: (i, 0))
    full = lambda a: pl.BlockSpec(a.shape, lambda i: (0,) * a.ndim)
    return pl.pallas_call(
        _memkv_kernel, grid=(n // tile,), in_specs=[row, full(g), full(wk), full(wv)],
        out_specs=(row, row),
        out_shape=(jax.ShapeDtypeStruct((n, wk.shape[1]), F32), jax.ShapeDtypeStruct((n, wv.shape[1]), F32)),
        compiler_params=_params("parallel"), name="memkv",
    )(mem, g, wk, wv)


def _mix_xattn_kernel(x_ref, yc_ref, att_ref, wa_ref, wb_ref, g2_ref, wq_ref, mk_ref, mv_ref, wo_ref,
                      o_ref, *, groups, rows, xheads, xhd):
    y1 = x_ref[...] + _dot(yc_ref[...].astype(BF16), wa_ref[...]) + _dot(att_ref[...].astype(BF16), wb_ref[...])
    hb = _rms(y1, g2_ref[...]).astype(BF16)
    q = _dot(hb, wq_ref[...]) * (xhd ** -0.5)
    outs = []
    for g in range(groups):
        heads_out = []
        for h in range(xheads):
            cols = slice(h * xhd, (h + 1) * xhd)
            qg = q[g * rows:(g + 1) * rows, cols].astype(BF16)
            s = _dot_nt(qg, mk_ref[g, :, cols].astype(BF16))
            p = jnp.exp(s - jnp.max(s, axis=-1, keepdims=True))
            p = p / jnp.sum(p, axis=-1, keepdims=True)
            heads_out.append(_dot(p.astype(BF16), mv_ref[g, :, cols].astype(BF16)))
        outs.append(jnp.concatenate(heads_out, axis=-1))
    o = outs[0] if groups == 1 else jnp.concatenate(outs, axis=0)
    o_ref[...] = y1 + _dot(o.astype(BF16), wo_ref[...])


def _mix_xattn(x, yconv, att, wa, wb, g2, wq, mk, mv, wo, *, groups, rows, xheads):
    n, d = x.shape
    tile = groups * rows
    c = yconv.shape[1]
    a = att.shape[1]
    mem_len = mk.shape[1]
    tiles_per_mem = (n // tile) // (mk.shape[0] // groups)
    row = lambda w: pl.BlockSpec((tile, w), lambda i: (i, 0))
    full = lambda arr: pl.BlockSpec(arr.shape, lambda i: (0,) * arr.ndim)
    mspec = pl.BlockSpec((groups, mem_len, d), lambda i: (i // tiles_per_mem, 0, 0))
    return pl.pallas_call(
        functools.partial(_mix_xattn_kernel, groups=groups, rows=rows, xheads=xheads, xhd=d // xheads),
        grid=(n // tile,),
        in_specs=[row(d), row(c), row(a), full(wa), full(wb), full(g2), full(wq), mspec, mspec, full(wo)],
        out_specs=row(d), out_shape=jax.ShapeDtypeStruct((n, d), F32),
        compiler_params=_params("parallel"), name="mix_xattn",
    )(x, yconv, att, wa, wb, g2, wq, mk, mv, wo)


def _extract_topk(s, order, payload, k):
    vals, pays = [], []
    big = jnp.int32(2 ** 30)
    for _ in range(k):
        m = jnp.max(s, axis=0, keepdims=True)
        pos = jnp.min(jnp.where(s == m, order, big), axis=0, keepdims=True)
        hit = order == pos
        vals.append(m)
        pays.append(pos if payload is None else jnp.max(jnp.where(hit, payload, -1), axis=0, keepdims=True))
        s = jnp.where(hit, -jnp.inf, s)
    return jnp.concatenate(vals, axis=0), jnp.concatenate(pays, axis=0)


def _route_kernel(y_ref, g_ref, wq_ref, sk_ref, hn_ref, idx_ref, gate_ref, *, heads, nkeys, half, topk):
    hn = _rms(y_ref[...], g_ref[...])
    hn_ref[...] = hn
    q = _dot(hn.astype(BF16), wq_ref[...])
    t = q.shape[0]
    key_order = lax.broadcasted_iota(I32, (nkeys, t), 0)
    cand_order = lax.broadcasted_iota(I32, (topk * topk, t), 0)
    for h in range(heads):
        top_s, top_i = [], []
        for c in range(2):
            lo = (2 * h + c) * half
            s = _dot_nt(sk_ref[h, c], q[:, lo:lo + half].astype(BF16))
            vs, ids = _extract_topk(s, key_order, None, topk)
            top_s.append(vs)
            top_i.append(ids)
        cand_s = jnp.concatenate([top_s[0][a:a + 1] + top_s[1] for a in range(topk)], axis=0)
        cand_i = jnp.concatenate([top_i[0][a:a + 1] * nkeys + top_i[1] for a in range(topk)], axis=0)
        best_s, best_i = _extract_topk(cand_s, cand_order, cand_i, topk)
        e = jnp.exp(best_s - jnp.max(best_s, axis=0, keepdims=True))
        gate_ref[h * topk:(h + 1) * topk, :] = e / jnp.sum(e, axis=0, keepdims=True)
        idx_ref[h * topk:(h + 1) * topk, :] = best_i


def _route(y, g, wq, sk, *, topk, tile=256):
    n, d = y.shape
    heads, _, nkeys, half = sk.shape
    tile = min(tile, n)
    row = pl.BlockSpec((tile, d), lambda i: (i, 0))
    col = pl.BlockSpec((heads * topk, tile), lambda i: (0, i))
    full = lambda a: pl.BlockSpec(a.shape, lambda i: (0,) * a.ndim)
    return pl.pallas_call(
        functools.partial(_route_kernel, heads=heads, nkeys=nkeys, half=half, topk=topk),
        grid=(n // tile,), in_specs=[row, full(g), full(wq), full(sk)], out_specs=(row, col, col),
        out_shape=(jax.ShapeDtypeStruct((n, d), F32), jax.ShapeDtypeStruct((heads * topk, n), I32),
                   jax.ShapeDtypeStruct((heads * topk, n), F32)),
        compiler_params=_params("parallel"), name="peer_route",
    )(y, g, wq, sk)


def _experts_kernel(idx_hbm, hn_ref, gate_ref, y_ref, gf_ref, tab_hbm, o_ref,
                    idx_smem, rows, y3, isem, rsem, *, tt, ne, d, depth, final_norm):
    i = pl.program_id(0)
    nblk = pl.num_programs(0)
    slot = i % 2

    def idx_copy(blk, sl):
        return pltpu.make_async_copy(idx_hbm.at[blk], idx_smem.at[sl], isem.at[sl])

    @pl.when(i == 0)
    def _():
        idx_copy(0, 0).start()

    idx_copy(i, slot).wait()

    @pl.when(i + 1 < nblk)
    def _():
        idx_copy(i + 1, 1 - slot).start()

    def row_copy(src_row, rs, e):
        return pltpu.make_async_copy(tab_hbm.at[src_row], rows.at[rs, e], rsem.at[rs])

    def issue(t, rs):
        for e in range(ne):
            row_copy(idx_smem[slot, t * ne + e], rs, e).start(priority=e % 2)

    for t in range(depth):
        issue(t, t)

    def body(t, carry):
        rs = t % depth
        for e in range(ne):
            row_copy(0, rs, e).wait()
        x8 = jnp.broadcast_to(hn_ref[pl.ds(t, 1), :], (SUBLANES, d)).astype(BF16)
        h = _dot_nt(x8, rows[rs, :, 0:d].astype(BF16))
        act = 0.5 * h * (1.0 + lax.erf(h * (2.0 ** -0.5)))
        w8 = jnp.broadcast_to(gate_ref[pl.ds(t, 1), :], (SUBLANES, ne)) * act
        r = _dot(w8.astype(BF16), rows[rs, :, d:2 * d].astype(BF16))
        y3[pl.ds(t, 1), :] = y_ref[pl.ds(t, 1), :] + r[0:1, :]

        @pl.when(t + depth < tt)
        def _():
            issue(t + depth, rs)

        return carry

    lax.fori_loop(0, tt, body, 0)
    o_ref[...] = _rms(y3[...], gf_ref[...]) if final_norm else y3[...]


def _experts(idx, hn, gates, y, gf, table, *, final_norm, tt=64, depth=8):
    n, d = hn.shape
    ne = idx.shape[1]
    tt = min(tt, n)
    depth = min(depth, tt)
    idx_blocks = idx.reshape(n // tt, tt * ne)
    row = lambda w: pl.BlockSpec((tt, w), lambda i: (i, 0))
    anyspec = pl.BlockSpec(memory_space=pl.ANY)
    return pl.pallas_call(
        functools.partial(_experts_kernel, tt=tt, ne=ne, d=d, depth=depth, final_norm=final_norm),
        grid=(n // tt,),
        in_specs=[anyspec, row(d), row(ne), row(d), pl.BlockSpec(gf.shape, lambda i: (0, 0)), anyspec],
        out_specs=row(d), out_shape=jax.ShapeDtypeStruct((n, d), F32),
        scratch_shapes=[pltpu.SMEM((2, tt * ne), I32), pltpu.VMEM((depth, ne, 2 * d), F32),
                        pltpu.VMEM((tt, d), F32), pltpu.SemaphoreType.DMA((2,)),
                        pltpu.SemaphoreType.DMA((depth,))],
        compiler_params=_params("arbitrary"), name="peer_experts",
    )(idx_blocks, hn, gates, y, gf, table)


def _peer(y, g3, wq_b, sk_b, table, gf, *, topk, final_norm):
    hn, idx_t, gate_t = _route(y, g3, wq_b, sk_b, topk=topk)
    return _experts(idx_t.T, hn, gate_t.T, y, gf, table, final_norm=final_norm)


def kernel(x_prompt, x_sample, mem_prompt, cache_k, cache_v, cache_logf, page_table, state_conv, cache_mem_k, cache_mem_v, norm1_g, w_in, b_f, conv_w, conv_b, conv_ln_g, conv_ln_b, w_out, norm2_g, mem_norm_g, wq_x, wk_x, wv_x, wo_x, norm3_g, w_query, sub_keys, expert_u, expert_v, normf_g):
    depth = w_in.shape[0]
    bsz, seq, d = x_prompt.shape
    nb, t, _ = x_sample.shape
    heads, hd = cache_k.shape[3], cache_k.shape[4]
    fw = heads * hd
    cc = conv_w.shape[2]
    xheads = cache_mem_k.shape[3]
    mem_len = mem_prompt.shape[1]
    peer_heads = sub_keys.shape[1]
    topk = 16
    n_pool, page = cache_k.shape[1], cache_k.shape[2]
    row = lambda a: a.reshape(1, -1)

    yp = x_prompt.reshape(bsz * seq, d)
    ys = x_sample.reshape(nb * t, d)
    outs = {k: [] for k in ("kp", "vp", "fp", "cp", "mkp", "mvp", "ks", "vs", "fs", "cs")}
    gf = row(normf_g)
    for l in range(depth):
        last = l == depth - 1
        w_main = w_in[l, :, :2 * cc + 3 * fw].astype(BF16)
        w_f = jnp.pad(w_in[l, :, 2 * cc + 3 * fw:], ((0, 0), (0, LANES - heads))).astype(BF16)
        wa = w_out[l, :cc].astype(BF16)
        wb = w_out[l, cc:].astype(BF16)
        wq_b, wo_b = wq_x[l].astype(BF16), wo_x[l].astype(BF16)
        wpq_b = w_query[l].astype(BF16)
        sk_b = sub_keys[l].astype(BF16)
        table = jnp.concatenate([expert_u[l], expert_v[l]], axis=1)
        cw, cb, lg, lb = conv_w[l], row(conv_b[l]), row(conv_ln_g[l]), row(conv_ln_b[l])
        inproj = functools.partial(_inproj, g=row(norm1_g[l]), w_main=w_main, w_f=w_f, b_f=row(b_f[l]),
                                   cc=cc, fw=fw, heads=heads, head_dim=hd)

        glu, qb, k, v, kb, vb, lf = inproj(ys)
        att = _fox_sample(page_table, qb.astype(F32), k, v, lf,
                          cache_k[l].reshape(n_pool, page, fw), cache_v[l].reshape(n_pool, page, fw),
                          cache_logf[l], t=t, heads=heads, hd=hd)
        yconv, new_state = _conv_sample(state_conv[l], glu.reshape(nb, t, cc), cw, cb, lg, lb)
        groups = min(SAMPLE_GROUPS, nb)
        y2 = _mix_xattn(ys, yconv.reshape(nb * t, cc), att, wa, wb, row(norm2_g[l]), wq_b,
                        cache_mem_k[l].reshape(nb, mem_len, d), cache_mem_v[l].reshape(nb, mem_len, d), wo_b,
                        groups=groups, rows=t, xheads=xheads)
        ys = _peer(y2, row(norm3_g[l]), wpq_b, sk_b, table, gf, topk=topk, final_norm=last)
        outs["ks"].append(k.reshape(nb, t, heads, hd))
        outs["vs"].append(v.reshape(nb, t, heads, hd))
        outs["fs"].append(lf.reshape(nb, t, heads))
        outs["cs"].append(new_state)

        glu, qb, k, v, kb, vb, lf = inproj(yp)
        c_t = _cumsum_lanes(lf.reshape(bsz, seq, heads).transpose(0, 2, 1))
        att = _fox_prompt(qb.reshape(bsz, seq, fw), kb.reshape(bsz, seq, fw), vb.reshape(bsz, seq, fw),
                          c_t.reshape(bsz, heads // 2, 2, seq), hd=hd)
        glu3 = glu.reshape(bsz, seq, cc)
        yconv = _conv_prompt(glu3, cw, cb, lg, lb)
        mk, mv = _memkv(mem_prompt.reshape(bsz * mem_len, d), row(mem_norm_g[l]),
                        wk_x[l].astype(BF16), wv_x[l].astype(BF16))
        y2 = _mix_xattn(yp, yconv.reshape(bsz * seq, cc), att.reshape(bsz * seq, fw), wa, wb, row(norm2_g[l]),
                        wq_b, mk.reshape(bsz, mem_len, d), mv.reshape(bsz, mem_len, d), wo_b,
                        groups=1, rows=min(512, seq), xheads=xheads)
        yp = _peer(y2, row(norm3_g[l]), wpq_b, sk_b, table, gf, topk=topk, final_norm=last)
        outs["kp"].append(k.reshape(bsz, seq, heads, hd))
        outs["vp"].append(v.reshape(bsz, seq, heads, hd))
        outs["fp"].append(lf.reshape(bsz, seq, heads))
        outs["cp"].append(glu3[:, seq - (cw.shape[0] - 1):])
        outs["mkp"].append(mk.reshape(bsz, mem_len, xheads, d // xheads))
        outs["mvp"].append(mv.reshape(bsz, mem_len, xheads, d // xheads))
    st = lambda key: jnp.stack(outs[key])
    return (yp.reshape(bsz, seq, d), ys.reshape(nb, t, d), st("kp"), st("vp"), st("fp"), st("cp"),
            st("mkp"), st("mvp"), st("ks"), st("vs"), st("fs"), st("cs"))
```

```python
import functools

import jax
import jax.numpy as jnp
from jax import lax
from jax.experimental import pallas as pl
from jax.experimental.pallas import tpu as pltpu

F32 = jnp.float32
BF16 = jnp.bfloat16
I32 = jnp.int32

EPS = 1e-6
NEG_INF = -1e30
LANES = 128
SUBLANES = 8
VMEM_LIMIT = 48 * 1024 * 1024
SAMPLE_GROUPS = 4
IDX_SLOTS = 3


def _params(*sem):
    return pltpu.CompilerParams(dimension_semantics=sem, vmem_limit_bytes=VMEM_LIMIT)


def _dot(a, b):
    return jnp.dot(a, b, preferred_element_type=F32)


def _dot_nt(a, b):
    return lax.dot_general(a, b, (((1,), (1,)), ((), ())), preferred_element_type=F32)


def _rms(x, g):
    return x * lax.rsqrt(jnp.mean(x * x, axis=-1, keepdims=True) + EPS) * g


def _split3(x):
    hi = x.astype(BF16)
    r = x - hi.astype(F32)
    mid = r.astype(BF16)
    lo = (r - mid.astype(F32)).astype(BF16)
    return hi, mid, lo


def _inproj_kernel(x_ref, g_ref, w_ref, wf_ref, bf_ref,
                   glu_ref, q_ref, k_ref, v_ref, kb_ref, vb_ref, lf_ref, *, cc, fw, heads, qscale):
    xb = _rms(x_ref[...], g_ref[...]).astype(BF16)

    def mm(lo, n):
        return _dot(xb, w_ref[:, lo:lo + n])

    glu_ref[...] = mm(0, cc) * jax.nn.sigmoid(mm(cc, cc))
    q_ref[...] = (mm(2 * cc, fw) * qscale).astype(BF16)
    k = mm(2 * cc + fw, fw)
    k_ref[...] = k
    kb_ref[...] = k.astype(BF16)
    v = mm(2 * cc + 2 * fw, fw)
    v_ref[...] = v
    vb_ref[...] = v.astype(BF16)
    z = _dot(xb, wf_ref[...])[:, :heads] + bf_ref[...]
    lf_ref[...] = jnp.minimum(z, 0.0) - jnp.log1p(jnp.exp(-jnp.abs(z)))


def _inproj(x, g, w_main, w_f, b_f, *, cc, fw, heads, head_dim, tile=512):
    n, d = x.shape
    tile = min(tile, n)
    row = lambda w: pl.BlockSpec((tile, w), lambda i: (i, 0))
    full = lambda a: pl.BlockSpec(a.shape, lambda i: (0,) * a.ndim)
    out_shape = (jax.ShapeDtypeStruct((n, cc), F32), jax.ShapeDtypeStruct((n, fw), BF16),
                 jax.ShapeDtypeStruct((n, fw), F32), jax.ShapeDtypeStruct((n, fw), F32),
                 jax.ShapeDtypeStruct((n, fw), BF16), jax.ShapeDtypeStruct((n, fw), BF16),
                 jax.ShapeDtypeStruct((n, heads), F32))
    return pl.pallas_call(
        functools.partial(_inproj_kernel, cc=cc, fw=fw, heads=heads, qscale=head_dim ** -0.5),
        grid=(n // tile,),
        in_specs=[row(d), full(g), full(w_main), full(w_f), full(b_f)],
        out_specs=(row(cc), row(fw), row(fw), row(fw), row(fw), row(fw), row(heads)),
        out_shape=out_shape, compiler_params=_params("parallel"), name="inproj",
    )(x, g, w_main, w_f, b_f)


def _cumsum_kernel(x_ref, hi_ref, mid_ref, lo_ref, *, nchunk):
    h = x_ref.shape[1]
    lane = lax.broadcasted_iota(I32, (h, LANES), 1)

    def body(i, carry):
        off = pl.multiple_of(i * LANES, LANES)
        y = x_ref[0, :, pl.ds(off, LANES)]
        d = 1
        while d < LANES:
            y = y + jnp.where(lane >= d, pltpu.roll(y, d, axis=1), 0.0)
            d *= 2
        y = y + carry
        for ref, piece in zip((hi_ref, mid_ref, lo_ref), _split3(-y)):
            ref[0, :, pl.ds(off, LANES)] = piece.astype(F32)
        return y[:, LANES - 1:LANES]

    lax.fori_loop(0, nchunk, body, jnp.zeros((h, 1), F32))


def _neg_cumsum_pieces(x):
    b, h, s = x.shape
    spec = pl.BlockSpec((1, h, s), lambda i: (i, 0, 0))
    shape = jax.ShapeDtypeStruct(x.shape, F32)
    return pl.pallas_call(
        functools.partial(_cumsum_kernel, nchunk=s // LANES), grid=(b,),
        in_specs=[spec], out_specs=(spec, spec, spec), out_shape=(shape, shape, shape),
        compiler_params=_params("parallel"), name="cumsum",
    )(x)


def _fox_prompt_kernel(q_ref, k_ref, v_ref, o_ref, m_ref, l_ref, acc_ref, *, tq, tk):
    i = pl.program_id(2)
    nh = q_ref.shape[1]
    m_ref[...] = jnp.full(m_ref.shape, NEG_INF, F32)
    l_ref[...] = jnp.zeros(l_ref.shape, F32)
    acc_ref[...] = jnp.zeros(acc_ref.shape, F32)

    def step(j, masked):
        off = pl.multiple_of(j * tk, tk)
        ss = [_dot(k_ref[0, h, pl.ds(off, tk), :], q_ref[0, h]) for h in range(nh)]
        ps, alphas = [], []
        for h, s in enumerate(ss):
            if masked:
                key = off + lax.broadcasted_iota(I32, (tk, tq), 0)
                qry = i * tq + lax.broadcasted_iota(I32, (tk, tq), 1)
                s = jnp.where(key <= qry, s, NEG_INF)
            m_old = m_ref[h]
            m_new = jnp.maximum(m_old, jnp.max(s, axis=0, keepdims=True))
            p = jnp.exp(s - m_new)
            alpha = jnp.exp(m_old - m_new)
            l_ref[h] = alpha * l_ref[h] + jnp.sum(p, axis=0, keepdims=True)
            m_ref[h] = m_new
            ps.append(p.astype(BF16))
            alphas.append(alpha)
        pvs = [_dot(v_ref[0, h, :, pl.ds(off, tk)], ps[h]) for h in range(nh)]
        for h in range(nh):
            acc_ref[h] = acc_ref[h] * alphas[h] + pvs[h]

    def body(j, carry):
        step(j, False)
        return carry

    nfull = i * (tq // tk)
    lax.fori_loop(0, nfull, body, 0)
    for jj in range(tq // tk):
        step(nfull + jj, True)
    for h in range(nh):
        o_ref[0, h] = (acc_ref[h] / l_ref[h]).astype(o_ref.dtype)


def _fox_prompt(q_aug, k_aug, v_t, *, tq=512, tk=256, nh=4):
    b, heads, aug, s = q_aug.shape
    hd = v_t.shape[2]
    tq = min(tq, s)
    tk = min(tk, tq)
    qspec = pl.BlockSpec((1, nh, aug, tq), lambda bi, p, i: (bi, p, 0, i))
    once = dict(pipeline_mode=pl.Buffered(1))
    kspec = pl.BlockSpec((1, nh, s, aug), lambda bi, p, i: (bi, p, 0, 0), **once)
    vspec = pl.BlockSpec((1, nh, hd, s), lambda bi, p, i: (bi, p, 0, 0), **once)
    ospec = pl.BlockSpec((1, nh, hd, tq), lambda bi, p, i: (bi, p, 0, i))
    return pl.pallas_call(
        functools.partial(_fox_prompt_kernel, tq=tq, tk=tk),
        grid=(b, heads // nh, s // tq),
        in_specs=[qspec, kspec, vspec], out_specs=ospec,
        out_shape=jax.ShapeDtypeStruct((b, heads, hd, s), BF16),
        scratch_shapes=[pltpu.VMEM((nh, 1, tq), F32), pltpu.VMEM((nh, 1, tq), F32),
                        pltpu.VMEM((nh, hd, tq), F32)],
        compiler_params=_params("parallel", "parallel", "arbitrary"), name="fox_prompt",
    )(q_aug, k_aug, v_t)


def _fox_sample_kernel(pt_ref, q_ref, kn_ref, vn_ref, lfn_ref, ck_hbm, cv_hbm, clf_hbm, o_ref,
                       kbuf, vbuf, lfbuf, knew, vnew, m_ref, l_ref, acc_ref, pre_ref, sem,
                       *, nb, t, heads, hd, pg, page, ngroups):
    rows = heads * t
    width = heads * hd

    def copies(step, slot):
        b = step // ngroups
        g = step % ngroups
        out = []
        for i in range(pg):
            pid = pt_ref[b, g * pg + i]
            out.append(pltpu.make_async_copy(ck_hbm.at[pid], kbuf.at[slot, i], sem.at[slot, 0]))
            out.append(pltpu.make_async_copy(cv_hbm.at[pid], vbuf.at[slot, i], sem.at[slot, 1]))
            out.append(pltpu.make_async_copy(clf_hbm.at[pid], lfbuf.at[slot, i], sem.at[slot, 2]))
        return out

    knew[...] = jnp.zeros(knew.shape, F32)
    vnew[...] = jnp.zeros(vnew.shape, F32)

    r_i = lax.broadcasted_iota(I32, (rows, width), 0)
    c_i = lax.broadcasted_iota(I32, (rows, width), 1)
    blockdiag = (r_i // t) == (c_i // hd)
    u_r = lax.broadcasted_iota(I32, (page, page), 0)
    u_c = lax.broadcasted_iota(I32, (page, page), 1)
    tri = jnp.where(u_r <= u_c, 1.0, 0.0).astype(BF16)
    n_r = lax.broadcasted_iota(I32, (rows, page), 0)
    n_c = lax.broadcasted_iota(I32, (rows, page), 1)
    new_mask = n_c <= (n_r % t)

    def pages_update(s_raws, lf_ts, mask, pv_fns):
        carry = pre_ref[...]
        ss = []
        for s_raw, lf_t in zip(s_raws, lf_ts):
            lfx = jnp.concatenate([jnp.broadcast_to(lf_t[h:h + 1, :], (t, page)) for h in range(heads)], axis=0)
            pre = carry
            for piece in _split3(lfx):
                pre = pre + _dot(piece, tri)
            carry = pre[:, page - 1:page]
            ss.append(s_raw - pre)
        pre_ref[...] = carry
        s = ss[0] if len(ss) == 1 else jnp.concatenate(ss, axis=1)
        if mask is not None:
            s = jnp.where(mask, s, NEG_INF)
        m_old = m_ref[...]
        m_new = jnp.maximum(m_old, jnp.max(s, axis=-1, keepdims=True))
        pf = jnp.exp(s - m_new)
        alpha = jnp.exp(m_old - m_new)
        l_ref[...] = alpha * l_ref[...] + jnp.sum(pf, axis=-1, keepdims=True)
        m_ref[...] = m_new
        p = pf.astype(BF16)
        pv = pv_fns[0](p[:, 0:page])
        for i in range(1, len(pv_fns)):
            pv = pv + pv_fns[i](p[:, i * page:(i + 1) * page])
        acc_ref[...] = alpha * acc_ref[...] + pv

    for c in copies(0, 0):
        c.start()

    def body(step, carry):
        slot = step % 2
        b = step // ngroups
        g = step % ngroups
        row0 = pl.multiple_of(b * t, t)

        @pl.when(step + 1 < nb * ngroups)
        def _():
            for c in copies(step + 1, 1 - slot):
                c.start()

        @pl.when(g == 0)
        def _():
            m_ref[...] = jnp.full(m_ref.shape, NEG_INF, F32)
            l_ref[...] = jnp.zeros(l_ref.shape, F32)
            acc_ref[...] = jnp.zeros(acc_ref.shape, F32)
            pre_ref[...] = jnp.zeros(pre_ref.shape, F32)

        q8 = q_ref[pl.ds(row0, t), :]
        qbd = jnp.where(blockdiag, jnp.concatenate([q8] * heads, axis=0), 0.0).astype(BF16)

        for c in copies(step, slot):
            c.wait()
        pages_update([_dot(qbd, kbuf[slot, i].astype(BF16)) for i in range(pg)],
                     [lfbuf[slot, i] for i in range(pg)], None,
                     [lambda p, i=i: _dot_nt(p, vbuf[slot, i].astype(BF16)) for i in range(pg)])

        @pl.when(g == ngroups - 1)
        def _():
            knew[0:t, :] = kn_ref[pl.ds(row0, t), :]
            vnew[0:t, :] = vn_ref[pl.ds(row0, t), :]
            v_n = vnew[...].astype(BF16)
            pages_update([_dot_nt(qbd, knew[...].astype(BF16))], [lfn_ref[b]], new_mask,
                         [lambda p: _dot(p, v_n)])
            o = jnp.where(blockdiag, acc_ref[...] / l_ref[...], 0.0)
            out = o[0:t]
            for h in range(1, heads):
                out = out + o[h * t:(h + 1) * t]
            o_ref[pl.ds(row0, t), :] = out

        return carry

    lax.fori_loop(0, nb * ngroups, body, 0)


def _fox_sample(page_table, q, k_new, v_new, lf_new_t, cache_kt, cache_vt, cache_lft, *, t, heads, hd, pg=8):
    nb, n_pages = page_table.shape
    page = cache_kt.shape[2]
    width = heads * hd
    rows = heads * t
    pg = min(pg, n_pages)
    ngroups = n_pages // pg
    vm = lambda a: pl.BlockSpec(a.shape, lambda i, pt: (0,) * a.ndim)
    anyspec = pl.BlockSpec(memory_space=pl.ANY)
    grid_spec = pltpu.PrefetchScalarGridSpec(
        num_scalar_prefetch=1, grid=(1,),
        in_specs=[vm(q), vm(k_new), vm(v_new), vm(lf_new_t), anyspec, anyspec, anyspec],
        out_specs=pl.BlockSpec((nb * t, width), lambda i, pt: (0, 0)),
        scratch_shapes=[pltpu.VMEM((2, pg, width, page), F32), pltpu.VMEM((2, pg, width, page), F32),
                        pltpu.VMEM((2, pg, heads, page), F32),
                        pltpu.VMEM((page, width), F32), pltpu.VMEM((page, width), F32),
                        pltpu.VMEM((rows, 1), F32), pltpu.VMEM((rows, 1), F32),
                        pltpu.VMEM((rows, width), F32), pltpu.VMEM((rows, 1), F32),
                        pltpu.SemaphoreType.DMA((2, 3))])
    return pl.pallas_call(
        functools.partial(_fox_sample_kernel, nb=nb, t=t, heads=heads, hd=hd, pg=pg, page=page,
                          ngroups=ngroups),
        grid_spec=grid_spec, out_shape=jax.ShapeDtypeStruct((nb * t, width), F32),
        compiler_params=_params("arbitrary"), name="fox_sample",
    )(page_table, q, k_new, v_new, lf_new_t, cache_kt, cache_vt, cache_lft)


def _ln_silu(y, conv_b, ln_g, ln_b):
    y = y + conv_b
    yc = y - jnp.mean(y, axis=-1, keepdims=True)
    y = yc * lax.rsqrt(jnp.mean(yc * yc, axis=-1, keepdims=True) + EPS) * ln_g + ln_b
    return y * jax.nn.sigmoid(y)


def _conv_prompt_kernel(cur_ref, halo_ref, w_ref, cb_ref, lg_ref, lb_ref, o_ref, ext, *, tile, halo, taps, chunk):
    i = pl.program_id(1)
    ext[0:halo, :] = jnp.where(i > 0, halo_ref[0], 0.0)
    ext[halo:halo + tile, :] = cur_ref[0]
    base = halo - (taps - 1)
    for r0 in range(0, tile, chunk):
        acc = jnp.zeros((chunk, ext.shape[1]), F32)
        for j in range(taps):
            acc = acc + w_ref[j:j + 1, :] * ext[r0 + base + j:r0 + base + j + chunk, :]
        o_ref[0, r0:r0 + chunk, :] = _ln_silu(acc, cb_ref[...], lg_ref[...], lb_ref[...])


def _conv_prompt(glu, conv_w, conv_b, ln_g, ln_b, *, tile=512, halo=32, chunk=64):
    b, s, c = glu.shape
    taps = conv_w.shape[0]
    tile = min(tile, s)
    per = tile // halo
    cur = pl.BlockSpec((1, tile, c), lambda bi, i: (bi, i, 0))
    hal = pl.BlockSpec((1, halo, c), lambda bi, i: (bi, jnp.maximum(i * per - 1, 0), 0))
    full = lambda a: pl.BlockSpec(a.shape, lambda bi, i: (0,) * a.ndim)
    return pl.pallas_call(
        functools.partial(_conv_prompt_kernel, tile=tile, halo=halo, taps=taps, chunk=chunk),
        grid=(b, s // tile),
        in_specs=[cur, hal, full(conv_w), full(conv_b), full(ln_g), full(ln_b)],
        out_specs=cur, out_shape=jax.ShapeDtypeStruct((b, s, c), F32),
        scratch_shapes=[pltpu.VMEM((halo + tile, c), F32)],
        compiler_params=_params("parallel", "parallel"), name="conv_prompt",
    )(glu, glu, conv_w, conv_b, ln_g, ln_b)


def _conv_sample_kernel(st_ref, glu_ref, w_ref, cb_ref, lg_ref, lb_ref, o_ref, ns_ref, ext, *, bt, t, taps):
    hist = taps - 1
    for b in range(bt):
        ext[0:hist, :] = st_ref[b]
        ext[hist:hist + t, :] = glu_ref[b]
        acc = jnp.zeros((t, ext.shape[1]), F32)
        for j in range(taps):
            acc = acc + w_ref[j:j + 1, :] * ext[j:j + t, :]
        o_ref[b] = _ln_silu(acc, cb_ref[...], lg_ref[...], lb_ref[...])
        ns_ref[b] = ext[t:t + hist, :]


def _conv_sample(state, glu, conv_w, conv_b, ln_g, ln_b, *, bt=8):
    nb, hist, c = state.shape
    t = glu.shape[1]
    taps = conv_w.shape[0]
    bt = min(bt, nb)
    st = pl.BlockSpec((bt, hist, c), lambda i: (i, 0, 0))
    gl = pl.BlockSpec((bt, t, c), lambda i: (i, 0, 0))
    full = lambda a: pl.BlockSpec(a.shape, lambda i: (0,) * a.ndim)
    return pl.pallas_call(
        functools.partial(_conv_sample_kernel, bt=bt, t=t, taps=taps), grid=(nb // bt,),
        in_specs=[st, gl, full(conv_w), full(conv_b), full(ln_g), full(ln_b)],
        out_specs=(gl, st),
        out_shape=(jax.ShapeDtypeStruct((nb, t, c), F32), jax.ShapeDtypeStruct((nb, hist, c), F32)),
        scratch_shapes=[pltpu.VMEM((hist + t + 2, c), F32)],
        compiler_params=_params("parallel"), name="conv_sample",
    )(state, glu, conv_w, conv_b, ln_g, ln_b)


def _memkv_kernel(m_ref, g_ref, wk_ref, wv_ref, k_ref, v_ref):
    mb = _rms(m_ref[...], g_ref[...]).astype(BF16)
    k_ref[...] = _dot(mb, wk_ref[...])
    v_ref[...] = _dot(mb, wv_ref[...])


def _memkv(mem, g, wk, wv, *, tile=256):
    n, d = mem.shape
    tile = min(tile, n)
    row = pl.BlockSpec((tile, d), lambda i: (i, 0))
    full = lambda a: pl.BlockSpec(a.shape, lambda i: (0,) * a.ndim)
    return pl.pallas_call(
        _memkv_kernel, grid=(n // tile,), in_specs=[row, full(g), full(wk), full(wv)],
        out_specs=(row, row),
        out_shape=(jax.ShapeDtypeStruct((n, wk.shape[1]), F32), jax.ShapeDtypeStruct((n, wv.shape[1]), F32)),
        compiler_params=_params("parallel"), name="memkv",
    )(mem, g, wk, wv)


def _mix_xattn_kernel(x_ref, yc_ref, att_ref, wa_ref, wb_ref, g2_ref, wq_ref, mk_ref, mv_ref, wo_ref,
                      o_ref, *, groups, rows, xheads, xhd):
    y1 = x_ref[...] + _dot(yc_ref[...].astype(BF16), wa_ref[...]) + _dot(att_ref[...].astype(BF16), wb_ref[...])
    hb = _rms(y1, g2_ref[...]).astype(BF16)
    q = _dot(hb, wq_ref[...]) * (xhd ** -0.5)
    outs = []
    for g in range(groups):
        heads_out = []
        for h in range(xheads):
            cols = slice(h * xhd, (h + 1) * xhd)
            qg = q[g * rows:(g + 1) * rows, cols].astype(BF16)
            s = _dot_nt(qg, mk_ref[g, :, cols].astype(BF16))
            p = jnp.exp(s - jnp.max(s, axis=-1, keepdims=True))
            p = p / jnp.sum(p, axis=-1, keepdims=True)
            heads_out.append(_dot(p.astype(BF16), mv_ref[g, :, cols].astype(BF16)))
        outs.append(jnp.concatenate(heads_out, axis=-1))
    o = outs[0] if groups == 1 else jnp.concatenate(outs, axis=0)
    o_ref[...] = y1 + _dot(o.astype(BF16), wo_ref[...])


def _mix_xattn(x, yconv, att, wa, wb, g2, wq, mk, mv, wo, *, groups, rows, xheads):
    n, d = x.shape
    tile = groups * rows
    c = yconv.shape[1]
    a = att.shape[1]
    mem_len = mk.shape[1]
    tiles_per_mem = (n // tile) // (mk.shape[0] // groups)
    row = lambda w: pl.BlockSpec((tile, w), lambda i: (i, 0))
    full = lambda arr: pl.BlockSpec(arr.shape, lambda i: (0,) * arr.ndim)
    mspec = pl.BlockSpec((groups, mem_len, d), lambda i: (i // tiles_per_mem, 0, 0))
    return pl.pallas_call(
        functools.partial(_mix_xattn_kernel, groups=groups, rows=rows, xheads=xheads, xhd=d // xheads),
        grid=(n // tile,),
        in_specs=[row(d), row(c), row(a), full(wa), full(wb), full(g2), full(wq), mspec, mspec, full(wo)],
        out_specs=row(d), out_shape=jax.ShapeDtypeStruct((n, d), F32),
        compiler_params=_params("parallel"), name="mix_xattn",
    )(x, yconv, att, wa, wb, g2, wq, mk, mv, wo)


def _extract_topk(s, order, payload, k):
    vals, pays = [], []
    big = jnp.int32(2 ** 30)
    for _ in range(k):
        m = jnp.max(s, axis=0, keepdims=True)
        pos = jnp.min(jnp.where(s == m, order, big), axis=0, keepdims=True)
        hit = order == pos
        vals.append(m)
        pays.append(pos if payload is None else jnp.max(jnp.where(hit, payload, -1), axis=0, keepdims=True))
        s = jnp.where(hit, -jnp.inf, s)
    return jnp.concatenate(vals, axis=0), jnp.concatenate(pays, axis=0)


def _route_kernel(y_ref, g_ref, wq_ref, sk_ref, hn_ref, idx_ref, gate_ref, *, heads, nkeys, half, topk):
    hn = _rms(y_ref[...], g_ref[...])
    hn_ref[...] = hn
    q = _dot(hn.astype(BF16), wq_ref[...])
    t = q.shape[0]
    key_order = lax.broadcasted_iota(I32, (nkeys, t), 0)
    sub = lax.broadcasted_iota(I32, (SUBLANES, t), 0)
    groups = [(0, 0, 8), (0, 8, 8)] + [(a, 0, min(8, topk // (a + 1))) for a in range(1, 8)]
    cand_order = jnp.concatenate([a * topk + b0 + sub for a, b0, _ in groups] + [(sub + 8) * topk], axis=0)
    for h in range(heads):
        top_s, top_i = [], []
        for c in range(2):
            lo = (2 * h + c) * half
            s = _dot_nt(sk_ref[h, c], q[:, lo:lo + half].astype(BF16))
            vs, ids = _extract_topk(s, key_order, None, topk)
            top_s.append(vs)
            top_i.append(ids)
        cs, ci = [], []
        for a, b0, nvalid in groups:
            s_ab = top_s[0][a:a + 1] + top_s[1][b0:b0 + SUBLANES]
            cs.append(s_ab if nvalid == SUBLANES else jnp.where(sub < nvalid, s_ab, -jnp.inf))
            ci.append(top_i[0][a:a + 1] * nkeys + top_i[1][b0:b0 + SUBLANES])
        cs.append(top_s[0][8:topk] + top_s[1][0:1])
        ci.append(top_i[0][8:topk] * nkeys + top_i[1][0:1])
        best_s, best_i = _extract_topk(jnp.concatenate(cs, axis=0), cand_order,
                                       jnp.concatenate(ci, axis=0), topk)
        e = jnp.exp(best_s - jnp.max(best_s, axis=0, keepdims=True))
        gate_ref[h * topk:(h + 1) * topk, :] = e / jnp.sum(e, axis=0, keepdims=True)
        idx_ref[h * topk:(h + 1) * topk, :] = best_i


def _route(y, g, wq, sk, *, topk, tile=256):
    n, d = y.shape
    heads, _, nkeys, half = sk.shape
    assert topk == 2 * SUBLANES, "the candidate grouping in _route_kernel is laid out for top-16"
    tile = min(tile, n)
    row = pl.BlockSpec((tile, d), lambda i: (i, 0))
    col = pl.BlockSpec((heads * topk, tile), lambda i: (0, i))
    full = lambda a: pl.BlockSpec(a.shape, lambda i: (0,) * a.ndim)
    return pl.pallas_call(
        functools.partial(_route_kernel, heads=heads, nkeys=nkeys, half=half, topk=topk),
        grid=(n // tile,), in_specs=[row, full(g), full(wq), full(sk)], out_specs=(row, col, col),
        out_shape=(jax.ShapeDtypeStruct((n, d), F32), jax.ShapeDtypeStruct((heads * topk, n), I32),
                   jax.ShapeDtypeStruct((heads * topk, n), F32)),
        compiler_params=_params("parallel"), name="peer_route",
    )(y, g, wq, sk)


def _experts_kernel(idx_hbm, hn_ref, gate_ref, y_ref, gf_ref, tab_hbm, o_ref,
                    idx_smem, rows, y3, isem, rsem, *, tt, ne, d, depth, group, final_norm):
    i = pl.program_id(0)
    nblk = pl.num_programs(0)
    blk_len = tt * ne
    cur = (i % IDX_SLOTS) * blk_len
    nxt = jnp.where(i + 1 < nblk, (i + 1) % IDX_SLOTS, i % IDX_SLOTS) * blk_len

    def idx_copy(blk):
        sl = blk % IDX_SLOTS
        dst = idx_smem.at[pl.ds(pl.multiple_of(sl * blk_len, blk_len), blk_len)]
        return pltpu.make_async_copy(idx_hbm.at[blk], dst, isem.at[sl])

    def row_copy(src_row, rs, e):
        return pltpu.make_async_copy(tab_hbm.at[src_row], rows.at[rs, e // SUBLANES, :, e % SUBLANES, :],
                                     rsem.at[rs])

    def issue(base, rs):
        ids = idx_smem.at[pl.ds(base, ne)]
        for e in range(ne):
            row_copy(ids[e], rs, e).start(priority=e % 2)

    def wait_rows(rs):
        for e in range(ne):
            row_copy(0, rs, e).wait()

    @pl.when(i == 0)
    def _():
        idx_copy(0).start()

        @pl.when(nblk > 1)
        def _():
            idx_copy(1).start()

        idx_copy(0).wait()
        for t in range(depth):
            issue(t * ne, t)

    @pl.when(i + 1 < nblk)
    def _():
        idx_copy(i + 1).wait()

    @pl.when(i + 2 < nblk)
    def _():
        idx_copy(i + 2).start()

    nct = d // LANES

    token_lane = lax.broadcasted_iota(I32, (ne, tt), 1)

    def compute(t, rs):
        def cols(lo, hi):
            return jnp.concatenate([rows[rs, :, c].reshape(ne, LANES) for c in range(lo, hi)], axis=1)

        h = jnp.sum(cols(0, nct) * hn_ref[pl.ds(t, 1), :], axis=1, keepdims=True)
        act = 0.5 * h * (1.0 + lax.erf(h * (2.0 ** -0.5)))
        gate = jnp.sum(jnp.where(token_lane == t, gate_ref[0], 0.0), axis=1, keepdims=True)
        r = jnp.sum((gate * act) * cols(nct, 2 * nct), axis=0, keepdims=True)
        y3[pl.ds(t, 1), :] = y_ref[pl.ds(t, 1), :] + r

    def body(k, carry):
        for g0 in range(0, depth, group):
            slots = range(g0, g0 + group)
            for rs in slots:
                wait_rows(rs)
            for rs in slots:
                compute(k * depth + rs, rs)
            for rs in slots:
                ahead = (k + 1) * depth + rs
                issue(jnp.where(ahead < tt, cur + ahead * ne, nxt + (ahead - tt) * ne), rs)
        return carry

    lax.fori_loop(0, tt // depth, body, 0)

    @pl.when(i == nblk - 1)
    def _():
        for rs in range(depth):
            wait_rows(rs)

    o_ref[...] = _rms(y3[...], gf_ref[...]) if final_norm else y3[...]


def _experts(idx_t, hn, gate_t, y, gf, table, *, final_norm, tt=64, depth=8, group=4):
    n, d = hn.shape
    ne = idx_t.shape[0]
    tt = min(tt, n)
    assert tt % depth == 0 and depth % group == 0 and n % tt == 0
    idx_blocks = idx_t.T.reshape(n // tt, tt * ne)
    gates = gate_t.reshape(ne, n // tt, tt).transpose(1, 0, 2)
    row = lambda w: pl.BlockSpec((tt, w), lambda i: (i, 0))
    anyspec = pl.BlockSpec(memory_space=pl.ANY)
    return pl.pallas_call(
        functools.partial(_experts_kernel, tt=tt, ne=ne, d=d, depth=depth, group=group,
                          final_norm=final_norm),
        grid=(n // tt,),
        in_specs=[anyspec, row(d), pl.BlockSpec((1, ne, tt), lambda i: (i, 0, 0)), row(d),
                  pl.BlockSpec(gf.shape, lambda i: (0, 0)), anyspec],
        out_specs=row(d), out_shape=jax.ShapeDtypeStruct((n, d), F32),
        scratch_shapes=[pltpu.SMEM((IDX_SLOTS * tt * ne,), I32),
                        pltpu.VMEM((depth, ne // SUBLANES, 2 * d // LANES, SUBLANES, LANES), F32),
                        pltpu.VMEM((tt, d), F32), pltpu.SemaphoreType.DMA((IDX_SLOTS,)),
                        pltpu.SemaphoreType.DMA((depth,))],
        compiler_params=_params("arbitrary"), name="peer_experts",
    )(idx_blocks, hn, gates, y, gf, table)


def _peer(y, g3, wq_b, sk_b, table, gf, *, topk, final_norm):
    hn, idx_t, gate_t = _route(y, g3, wq_b, sk_b, topk=topk)
    return _experts(idx_t, hn, gate_t, y, gf, table, final_norm=final_norm)


def kernel(x_prompt, x_sample, mem_prompt, cache_k, cache_v, cache_logf, page_table, state_conv, cache_mem_k, cache_mem_v, norm1_g, w_in, b_f, conv_w, conv_b, conv_ln_g, conv_ln_b, w_out, norm2_g, mem_norm_g, wq_x, wk_x, wv_x, wo_x, norm3_g, w_query, sub_keys, expert_u, expert_v, normf_g):
    depth = w_in.shape[0]
    bsz, seq, d = x_prompt.shape
    nb, t, _ = x_sample.shape
    heads, hd = cache_k.shape[3], cache_k.shape[4]
    fw = heads * hd
    cc = conv_w.shape[2]
    xheads = cache_mem_k.shape[3]
    mem_len = mem_prompt.shape[1]
    peer_heads = sub_keys.shape[1]
    topk = 16
    n_pool, page = cache_k.shape[1], cache_k.shape[2]
    row = lambda a: a.reshape(1, -1)

    yp = x_prompt.reshape(bsz * seq, d)
    ys = x_sample.reshape(nb * t, d)
    outs = {k: [] for k in ("kp", "vp", "fp", "cp", "mkp", "mvp", "ks", "vs", "fs", "cs")}
    gf = row(normf_g)
    for l in range(depth):
        last = l == depth - 1
        w_main = w_in[l, :, :2 * cc + 3 * fw].astype(BF16)
        w_f = jnp.pad(w_in[l, :, 2 * cc + 3 * fw:], ((0, 0), (0, LANES - heads))).astype(BF16)
        wa = w_out[l, :cc].astype(BF16)
        wb = w_out[l, cc:].astype(BF16)
        wq_b, wo_b = wq_x[l].astype(BF16), wo_x[l].astype(BF16)
        wpq_b = w_query[l].astype(BF16)
        sk_b = sub_keys[l].astype(BF16)
        table = jnp.concatenate([expert_u[l], expert_v[l]], axis=1).reshape(-1, 2 * d // LANES, LANES)
        cw, cb, lg, lb = conv_w[l], row(conv_b[l]), row(conv_ln_g[l]), row(conv_ln_b[l])
        inproj = functools.partial(_inproj, g=row(norm1_g[l]), w_main=w_main, w_f=w_f, b_f=row(b_f[l]),
                                   cc=cc, fw=fw, heads=heads, head_dim=hd)

        glu, qb, k, v, kb, vb, lf = inproj(ys)
        keys_minor = lambda c: jnp.transpose(c, (0, 2, 3, 1)).reshape(n_pool, fw, page)
        lf_new_t = jnp.pad(lf.reshape(nb, t, heads).transpose(0, 2, 1), ((0, 0), (0, 0), (0, page - t)))
        att = _fox_sample(page_table, qb.astype(F32), k, v, lf_new_t,
                          keys_minor(cache_k[l]), keys_minor(cache_v[l]),
                          jnp.transpose(cache_logf[l], (0, 2, 1)), t=t, heads=heads, hd=hd)
        yconv, new_state = _conv_sample(state_conv[l], glu.reshape(nb, t, cc), cw, cb, lg, lb)
        groups = min(SAMPLE_GROUPS, nb)
        y2 = _mix_xattn(ys, yconv.reshape(nb * t, cc), att, wa, wb, row(norm2_g[l]), wq_b,
                        cache_mem_k[l].reshape(nb, mem_len, d), cache_mem_v[l].reshape(nb, mem_len, d), wo_b,
                        groups=groups, rows=t, xheads=xheads)
        ys = _peer(y2, row(norm3_g[l]), wpq_b, sk_b, table, gf, topk=topk, final_norm=last)
        outs["ks"].append(k.reshape(nb, t, heads, hd))
        outs["vs"].append(v.reshape(nb, t, heads, hd))
        outs["fs"].append(lf.reshape(nb, t, heads))
        outs["cs"].append(new_state)

        glu, qb, k, v, kb, vb, lf = inproj(yp)
        pieces = _neg_cumsum_pieces(lf.reshape(bsz, seq, heads).transpose(0, 2, 1))
        per_head = lambda a: a.reshape(bsz, seq, heads, hd).transpose(0, 2, 1, 3)
        pad = LANES - hd - len(pieces)
        k_aug = jnp.concatenate([per_head(kb), jnp.stack(pieces, axis=-1).astype(BF16),
                                 jnp.zeros((bsz, heads, seq, pad), BF16)], axis=-1)
        q_aug = jnp.concatenate([per_head(qb).transpose(0, 1, 3, 2),
                                 jnp.ones((bsz, heads, len(pieces), seq), BF16),
                                 jnp.zeros((bsz, heads, pad, seq), BF16)], axis=2)
        att_t = _fox_prompt(q_aug, k_aug, per_head(vb).transpose(0, 1, 3, 2))
        att = att_t.transpose(0, 3, 1, 2)
        glu3 = glu.reshape(bsz, seq, cc)
        yconv = _conv_prompt(glu3, cw, cb, lg, lb)
        mk, mv = _memkv(mem_prompt.reshape(bsz * mem_len, d), row(mem_norm_g[l]),
                        wk_x[l].astype(BF16), wv_x[l].astype(BF16))
        y2 = _mix_xattn(yp, yconv.reshape(bsz * seq, cc), att.reshape(bsz * seq, fw), wa, wb, row(norm2_g[l]),
                        wq_b, mk.reshape(bsz, mem_len, d), mv.reshape(bsz, mem_len, d), wo_b,
                        groups=1, rows=min(512, seq), xheads=xheads)
        yp = _peer(y2, row(norm3_g[l]), wpq_b, sk_b, table, gf, topk=topk, final_norm=last)
        outs["kp"].append(k.reshape(bsz, seq, heads, hd))
        outs["vp"].append(v.reshape(bsz, seq, heads, hd))
        outs["fp"].append(lf.reshape(bsz, seq, heads))
        outs["cp"].append(glu3[:, seq - (cw.shape[0] - 1):])
        outs["mkp"].append(mk.reshape(bsz, mem_len, xheads, d // xheads))
        outs["mvp"].append(mv.reshape(bsz, mem_len, xheads, d // xheads))
    st = lambda key: jnp.stack(outs[key])
    return (yp.reshape(bsz, seq, d), ys.reshape(nb, t, d), st("kp"), st("vp"), st("fp"), st("cp"),
            st("mkp"), st("mvp"), st("ks"), st("vs"), st("fs"), st("cs"))
```

```python
import functools

import jax
import jax.numpy as jnp
from jax import lax
from jax.experimental import pallas as pl
from jax.experimental.pallas import tpu as pltpu

F32 = jnp.float32
BF16 = jnp.bfloat16
I32 = jnp.int32

EPS = 1e-6
NEG_INF = -1e30
LANES = 128
SUBLANES = 8
VMEM_LIMIT = 48 * 1024 * 1024
SAMPLE_GROUPS = 4
IDX_SLOTS = 3
SLAB_STRIDE = 20


def _params(*sem):
    return pltpu.CompilerParams(dimension_semantics=sem, vmem_limit_bytes=VMEM_LIMIT)


def _dot(a, b):
    return jnp.dot(a, b, preferred_element_type=F32)


def _dot_nt(a, b):
    return lax.dot_general(a, b, (((1,), (1,)), ((), ())), preferred_element_type=F32)


def _rms(x, g):
    return x * lax.rsqrt(jnp.mean(x * x, axis=-1, keepdims=True) + EPS) * g


def _split3(x):
    hi = x.astype(BF16)
    r = x - hi.astype(F32)
    mid = r.astype(BF16)
    lo = (r - mid.astype(F32)).astype(BF16)
    return hi, mid, lo


def _inproj_kernel(x_ref, g_ref, w_ref, wf_ref, bf_ref,
                   glu_ref, q_ref, k_ref, v_ref, kb_ref, vb_ref, lf_ref, *, cc, fw, heads, qscale):
    xb = _rms(x_ref[...], g_ref[...]).astype(BF16)

    def mm(lo, n):
        return _dot(xb, w_ref[:, lo:lo + n])

    glu_ref[...] = mm(0, cc) * jax.nn.sigmoid(mm(cc, cc))
    q_ref[...] = (mm(2 * cc, fw) * qscale).astype(BF16)
    k = mm(2 * cc + fw, fw)
    k_ref[...] = k
    kb_ref[...] = k.astype(BF16)
    v = mm(2 * cc + 2 * fw, fw)
    v_ref[...] = v
    vb_ref[...] = v.astype(BF16)
    z = _dot(xb, wf_ref[...])[:, :heads] + bf_ref[...]
    lf_ref[...] = jnp.minimum(z, 0.0) - jnp.log1p(jnp.exp(-jnp.abs(z)))


def _inproj(x, g, w_main, w_f, b_f, *, cc, fw, heads, head_dim, tile=512):
    n, d = x.shape
    tile = min(tile, n)
    row = lambda w: pl.BlockSpec((tile, w), lambda i: (i, 0))
    full = lambda a: pl.BlockSpec(a.shape, lambda i: (0,) * a.ndim)
    out_shape = (jax.ShapeDtypeStruct((n, cc), F32), jax.ShapeDtypeStruct((n, fw), BF16),
                 jax.ShapeDtypeStruct((n, fw), F32), jax.ShapeDtypeStruct((n, fw), F32),
                 jax.ShapeDtypeStruct((n, fw), BF16), jax.ShapeDtypeStruct((n, fw), BF16),
                 jax.ShapeDtypeStruct((n, heads), F32))
    return pl.pallas_call(
        functools.partial(_inproj_kernel, cc=cc, fw=fw, heads=heads, qscale=head_dim ** -0.5),
        grid=(n // tile,),
        in_specs=[row(d), full(g), full(w_main), full(w_f), full(b_f)],
        out_specs=(row(cc), row(fw), row(fw), row(fw), row(fw), row(fw), row(heads)),
        out_shape=out_shape, compiler_params=_params("parallel"), name="inproj",
    )(x, g, w_main, w_f, b_f)


def _cumsum_kernel(x_ref, hi_ref, mid_ref, lo_ref, *, nchunk):
    h = x_ref.shape[1]
    lane = lax.broadcasted_iota(I32, (h, LANES), 1)

    def body(i, carry):
        off = pl.multiple_of(i * LANES, LANES)
        y = x_ref[0, :, pl.ds(off, LANES)]
        d = 1
        while d < LANES:
            y = y + jnp.where(lane >= d, pltpu.roll(y, d, axis=1), 0.0)
            d *= 2
        y = y + carry
        for ref, piece in zip((hi_ref, mid_ref, lo_ref), _split3(-y)):
            ref[0, :, pl.ds(off, LANES)] = piece.astype(F32)
        return y[:, LANES - 1:LANES]

    lax.fori_loop(0, nchunk, body, jnp.zeros((h, 1), F32))


def _neg_cumsum_pieces(x):
    b, h, s = x.shape
    spec = pl.BlockSpec((1, h, s), lambda i: (i, 0, 0))
    shape = jax.ShapeDtypeStruct(x.shape, F32)
    return pl.pallas_call(
        functools.partial(_cumsum_kernel, nchunk=s // LANES), grid=(b,),
        in_specs=[spec], out_specs=(spec, spec, spec), out_shape=(shape, shape, shape),
        compiler_params=_params("parallel"), name="cumsum",
    )(x)


def _fox_prompt_kernel(q_ref, k_ref, v_ref, o_ref, m_ref, l_ref, acc_ref, *, tq, tk):
    i = pl.program_id(2)
    nh = q_ref.shape[1]
    m_ref[...] = jnp.full(m_ref.shape, NEG_INF, F32)
    l_ref[...] = jnp.zeros(l_ref.shape, F32)
    acc_ref[...] = jnp.zeros(acc_ref.shape, F32)

    def step(j, masked):
        off = pl.multiple_of(j * tk, tk)
        ss = [_dot(k_ref[0, h, pl.ds(off, tk), :], q_ref[0, h]) for h in range(nh)]
        ps, alphas = [], []
        for h, s in enumerate(ss):
            if masked:
                key = off + lax.broadcasted_iota(I32, (tk, tq), 0)
                qry = i * tq + lax.broadcasted_iota(I32, (tk, tq), 1)
                s = jnp.where(key <= qry, s, NEG_INF)
            m_old = m_ref[h]
            m_new = jnp.maximum(m_old, jnp.max(s, axis=0, keepdims=True))
            p = jnp.exp(s - m_new)
            alpha = jnp.exp(m_old - m_new)
            l_ref[h] = alpha * l_ref[h] + jnp.sum(p, axis=0, keepdims=True)
            m_ref[h] = m_new
            ps.append(p.astype(BF16))
            alphas.append(alpha)
        pvs = [_dot(v_ref[0, h, :, pl.ds(off, tk)], ps[h]) for h in range(nh)]
        for h in range(nh):
            acc_ref[h] = acc_ref[h] * alphas[h] + pvs[h]

    def body(j, carry):
        step(j, False)
        return carry

    nfull = i * (tq // tk)
    lax.fori_loop(0, nfull, body, 0)
    for jj in range(tq // tk):
        step(nfull + jj, True)
    for h in range(nh):
        o_ref[0, h] = (acc_ref[h] / l_ref[h]).astype(o_ref.dtype)


def _fox_prompt(q_aug, k_aug, v_t, *, tq=512, tk=512, nh=4):
    b, heads, aug, s = q_aug.shape
    hd = v_t.shape[2]
    tq = min(tq, s)
    tk = min(tk, tq)
    qspec = pl.BlockSpec((1, nh, aug, tq), lambda bi, p, i: (bi, p, 0, i))
    once = dict(pipeline_mode=pl.Buffered(1))
    kspec = pl.BlockSpec((1, nh, s, aug), lambda bi, p, i: (bi, p, 0, 0), **once)
    vspec = pl.BlockSpec((1, nh, hd, s), lambda bi, p, i: (bi, p, 0, 0), **once)
    ospec = pl.BlockSpec((1, nh, hd, tq), lambda bi, p, i: (bi, p, 0, i))
    return pl.pallas_call(
        functools.partial(_fox_prompt_kernel, tq=tq, tk=tk),
        grid=(b, heads // nh, s // tq),
        in_specs=[qspec, kspec, vspec], out_specs=ospec,
        out_shape=jax.ShapeDtypeStruct((b, heads, hd, s), BF16),
        scratch_shapes=[pltpu.VMEM((nh, 1, tq), F32), pltpu.VMEM((nh, 1, tq), F32),
                        pltpu.VMEM((nh, hd, tq), F32)],
        compiler_params=_params("parallel", "parallel", "arbitrary"), name="fox_prompt",
    )(q_aug, k_aug, v_t)


def _fox_sample_kernel(pt_ref, q_ref, kn_ref, vn_ref, lfn_ref, ck_hbm, cv_hbm, clf_hbm, o_ref,
                       kbuf, vbuf, lfbuf, knew, vnew, m_ref, l_ref, acc_ref, pre_ref, sem,
                       *, nb, t, heads, hd, pg, page, ngroups):
    rows = heads * t
    width = heads * hd

    def copies(step, slot):
        b = step // ngroups
        g = step % ngroups
        out = []
        for i in range(pg):
            pid = pt_ref[b, g * pg + i]
            out.append(pltpu.make_async_copy(ck_hbm.at[pid], kbuf.at[slot, i], sem.at[slot, 0]))
            out.append(pltpu.make_async_copy(cv_hbm.at[pid], vbuf.at[slot, i], sem.at[slot, 1]))
            out.append(pltpu.make_async_copy(clf_hbm.at[pid], lfbuf.at[slot, i], sem.at[slot, 2]))
        return out

    knew[...] = jnp.zeros(knew.shape, F32)
    vnew[...] = jnp.zeros(vnew.shape, F32)

    r_i = lax.broadcasted_iota(I32, (rows, width), 0)
    c_i = lax.broadcasted_iota(I32, (rows, width), 1)
    blockdiag = (r_i // t) == (c_i // hd)
    u_r = lax.broadcasted_iota(I32, (page, page), 0)
    u_c = lax.broadcasted_iota(I32, (page, page), 1)
    tri = jnp.where(u_r <= u_c, 1.0, 0.0).astype(BF16)
    n_r = lax.broadcasted_iota(I32, (rows, page), 0)
    n_c = lax.broadcasted_iota(I32, (rows, page), 1)
    new_mask = n_c <= (n_r % t)

    def pages_update(s_raws, lf_ts, mask, pv_fns):
        carry = pre_ref[...]
        ss = []
        for s_raw, lf_t in zip(s_raws, lf_ts):
            lfx = jnp.concatenate([jnp.broadcast_to(lf_t[h:h + 1, :], (t, page)) for h in range(heads)], axis=0)
            pre = carry
            for piece in _split3(lfx):
                pre = pre + _dot(piece, tri)
            carry = pre[:, page - 1:page]
            ss.append(s_raw - pre)
        pre_ref[...] = carry
        s = ss[0] if len(ss) == 1 else jnp.concatenate(ss, axis=1)
        if mask is not None:
            s = jnp.where(mask, s, NEG_INF)
        m_old = m_ref[...]
        m_new = jnp.maximum(m_old, jnp.max(s, axis=-1, keepdims=True))
        pf = jnp.exp(s - m_new)
        alpha = jnp.exp(m_old - m_new)
        l_ref[...] = alpha * l_ref[...] + jnp.sum(pf, axis=-1, keepdims=True)
        m_ref[...] = m_new
        p = pf.astype(BF16)
        pv = pv_fns[0](p[:, 0:page])
        for i in range(1, len(pv_fns)):
            pv = pv + pv_fns[i](p[:, i * page:(i + 1) * page])
        acc_ref[...] = alpha * acc_ref[...] + pv

    for c in copies(0, 0):
        c.start()

    def body(step, carry):
        slot = step % 2
        b = step // ngroups
        g = step % ngroups
        row0 = pl.multiple_of(b * t, t)

        @pl.when(step + 1 < nb * ngroups)
        def _():
            for c in copies(step + 1, 1 - slot):
                c.start()

        @pl.when(g == 0)
        def _():
            m_ref[...] = jnp.full(m_ref.shape, NEG_INF, F32)
            l_ref[...] = jnp.zeros(l_ref.shape, F32)
            acc_ref[...] = jnp.zeros(acc_ref.shape, F32)
            pre_ref[...] = jnp.zeros(pre_ref.shape, F32)

        q8 = q_ref[pl.ds(row0, t), :]
        qbd = jnp.where(blockdiag, jnp.concatenate([q8] * heads, axis=0), 0.0).astype(BF16)

        for c in copies(step, slot):
            c.wait()
        pages_update([_dot(qbd, kbuf[slot, i].astype(BF16)) for i in range(pg)],
                     [lfbuf[slot, i] for i in range(pg)], None,
                     [lambda p, i=i: _dot_nt(p, vbuf[slot, i].astype(BF16)) for i in range(pg)])

        @pl.when(g == ngroups - 1)
        def _():
            knew[0:t, :] = kn_ref[pl.ds(row0, t), :]
            vnew[0:t, :] = vn_ref[pl.ds(row0, t), :]
            v_n = vnew[...].astype(BF16)
            pages_update([_dot_nt(qbd, knew[...].astype(BF16))], [lfn_ref[b]], new_mask,
                         [lambda p: _dot(p, v_n)])
            o = jnp.where(blockdiag, acc_ref[...] / l_ref[...], 0.0)
            out = o[0:t]
            for h in range(1, heads):
                out = out + o[h * t:(h + 1) * t]
            o_ref[pl.ds(row0, t), :] = out

        return carry

    lax.fori_loop(0, nb * ngroups, body, 0)


def _fox_sample(page_table, q, k_new, v_new, lf_new_t, cache_kt, cache_vt, cache_lft, *, t, heads, hd, pg=8):
    nb, n_pages = page_table.shape
    page = cache_kt.shape[2]
    width = heads * hd
    rows = heads * t
    pg = min(pg, n_pages)
    ngroups = n_pages // pg
    vm = lambda a: pl.BlockSpec(a.shape, lambda i, pt: (0,) * a.ndim)
    anyspec = pl.BlockSpec(memory_space=pl.ANY)
    grid_spec = pltpu.PrefetchScalarGridSpec(
        num_scalar_prefetch=1, grid=(1,),
        in_specs=[vm(q), vm(k_new), vm(v_new), vm(lf_new_t), anyspec, anyspec, anyspec],
        out_specs=pl.BlockSpec((nb * t, width), lambda i, pt: (0, 0)),
        scratch_shapes=[pltpu.VMEM((2, pg, width, page), F32), pltpu.VMEM((2, pg, width, page), F32),
                        pltpu.VMEM((2, pg, heads, page), F32),
                        pltpu.VMEM((page, width), F32), pltpu.VMEM((page, width), F32),
                        pltpu.VMEM((rows, 1), F32), pltpu.VMEM((rows, 1), F32),
                        pltpu.VMEM((rows, width), F32), pltpu.VMEM((rows, 1), F32),
                        pltpu.SemaphoreType.DMA((2, 3))])
    return pl.pallas_call(
        functools.partial(_fox_sample_kernel, nb=nb, t=t, heads=heads, hd=hd, pg=pg, page=page,
                          ngroups=ngroups),
        grid_spec=grid_spec, out_shape=jax.ShapeDtypeStruct((nb * t, width), F32),
        compiler_params=_params("arbitrary"), name="fox_sample",
    )(page_table, q, k_new, v_new, lf_new_t, cache_kt, cache_vt, cache_lft)


def _ln_silu(y, conv_b, ln_g, ln_b):
    y = y + conv_b
    yc = y - jnp.mean(y, axis=-1, keepdims=True)
    y = yc * lax.rsqrt(jnp.mean(yc * yc, axis=-1, keepdims=True) + EPS) * ln_g + ln_b
    return y * jax.nn.sigmoid(y)


def _conv_prompt_kernel(cur_ref, halo_ref, w_ref, cb_ref, lg_ref, lb_ref, o_ref, ext, *, tile, halo, taps, chunk):
    i = pl.program_id(1)
    ext[0:halo, :] = jnp.where(i > 0, halo_ref[0], 0.0)
    ext[halo:halo + tile, :] = cur_ref[0]
    base = halo - (taps - 1)
    for r0 in range(0, tile, chunk):
        acc = jnp.zeros((chunk, ext.shape[1]), F32)
        for j in range(taps):
            acc = acc + w_ref[j:j + 1, :] * ext[r0 + base + j:r0 + base + j + chunk, :]
        o_ref[0, r0:r0 + chunk, :] = _ln_silu(acc, cb_ref[...], lg_ref[...], lb_ref[...])


def _conv_prompt(glu, conv_w, conv_b, ln_g, ln_b, *, tile=512, halo=32, chunk=64):
    b, s, c = glu.shape
    taps = conv_w.shape[0]
    tile = min(tile, s)
    per = tile // halo
    cur = pl.BlockSpec((1, tile, c), lambda bi, i: (bi, i, 0))
    hal = pl.BlockSpec((1, halo, c), lambda bi, i: (bi, jnp.maximum(i * per - 1, 0), 0))
    full = lambda a: pl.BlockSpec(a.shape, lambda bi, i: (0,) * a.ndim)
    return pl.pallas_call(
        functools.partial(_conv_prompt_kernel, tile=tile, halo=halo, taps=taps, chunk=chunk),
        grid=(b, s // tile),
        in_specs=[cur, hal, full(conv_w), full(conv_b), full(ln_g), full(ln_b)],
        out_specs=cur, out_shape=jax.ShapeDtypeStruct((b, s, c), F32),
        scratch_shapes=[pltpu.VMEM((halo + tile, c), F32)],
        compiler_params=_params("parallel", "parallel"), name="conv_prompt",
    )(glu, glu, conv_w, conv_b, ln_g, ln_b)


def _conv_sample_kernel(st_ref, glu_ref, w_ref, cb_ref, lg_ref, lb_ref, o_ref, ns_ref, ext, *, bt, t, taps):
    hist = taps - 1
    for b in range(bt):
        ext[0:hist, :] = st_ref[b]
        ext[hist:hist + t, :] = glu_ref[b]
        acc = jnp.zeros((t, ext.shape[1]), F32)
        for j in range(taps):
            acc = acc + w_ref[j:j + 1, :] * ext[j:j + t, :]
        o_ref[b] = _ln_silu(acc, cb_ref[...], lg_ref[...], lb_ref[...])
        ns_ref[b] = ext[t:t + hist, :]


def _conv_sample(state, glu, conv_w, conv_b, ln_g, ln_b, *, bt=8):
    nb, hist, c = state.shape
    t = glu.shape[1]
    taps = conv_w.shape[0]
    bt = min(bt, nb)
    st = pl.BlockSpec((bt, hist, c), lambda i: (i, 0, 0))
    gl = pl.BlockSpec((bt, t, c), lambda i: (i, 0, 0))
    full = lambda a: pl.BlockSpec(a.shape, lambda i: (0,) * a.ndim)
    return pl.pallas_call(
        functools.partial(_conv_sample_kernel, bt=bt, t=t, taps=taps), grid=(nb // bt,),
        in_specs=[st, gl, full(conv_w), full(conv_b), full(ln_g), full(ln_b)],
        out_specs=(gl, st),
        out_shape=(jax.ShapeDtypeStruct((nb, t, c), F32), jax.ShapeDtypeStruct((nb, hist, c), F32)),
        scratch_shapes=[pltpu.VMEM((hist + t + 2, c), F32)],
        compiler_params=_params("parallel"), name="conv_sample",
    )(state, glu, conv_w, conv_b, ln_g, ln_b)


def _memkv_kernel(m_ref, g_ref, wk_ref, wv_ref, k_ref, v_ref):
    mb = _rms(m_ref[...], g_ref[...]).astype(BF16)
    k_ref[...] = _dot(mb, wk_ref[...])
    v_ref[...] = _dot(mb, wv_ref[...])


def _memkv(mem, g, wk, wv, *, tile=256):
    n, d = mem.shape
    tile = min(tile, n)
    row = pl.BlockSpec((tile, d), lambda i: (i, 0))
    full = lambda a: pl.BlockSpec(a.shape, lambda i: (0,) * a.ndim)
    return pl.pallas_call(
        _memkv_kernel, grid=(n // tile,), in_specs=[row, full(g), full(wk), full(wv)],
        out_specs=(row, row),
        out_shape=(jax.ShapeDtypeStruct((n, wk.shape[1]), F32), jax.ShapeDtypeStruct((n, wv.shape[1]), F32)),
        compiler_params=_params("parallel"), name="memkv",
    )(mem, g, wk, wv)


def _mix_xattn_kernel(x_ref, yc_ref, att_ref, wa_ref, wb_ref, g2_ref, wq_ref, mk_ref, mv_ref, wo_ref,
                      o_ref, *, groups, rows, xheads, xhd):
    y1 = x_ref[...] + _dot(yc_ref[...].astype(BF16), wa_ref[...]) + _dot(att_ref[...].astype(BF16), wb_ref[...])
    hb = _rms(y1, g2_ref[...]).astype(BF16)
    q = _dot(hb, wq_ref[...]) * (xhd ** -0.5)
    outs = []
    for g in range(groups):
        heads_out = []
        for h in range(xheads):
            cols = slice(h * xhd, (h + 1) * xhd)
            qg = q[g * rows:(g + 1) * rows, cols].astype(BF16)
            s = _dot_nt(qg, mk_ref[g, :, cols].astype(BF16))
            p = jnp.exp(s - jnp.max(s, axis=-1, keepdims=True))
            p = p / jnp.sum(p, axis=-1, keepdims=True)
            heads_out.append(_dot(p.astype(BF16), mv_ref[g, :, cols].astype(BF16)))
        outs.append(jnp.concatenate(heads_out, axis=-1))
    o = outs[0] if groups == 1 else jnp.concatenate(outs, axis=0)
    o_ref[...] = y1 + _dot(o.astype(BF16), wo_ref[...])


def _mix_xattn(x, yconv, att, wa, wb, g2, wq, mk, mv, wo, *, groups, rows, xheads):
    n, d = x.shape
    tile = groups * rows
    c = yconv.shape[1]
    a = att.shape[1]
    mem_len = mk.shape[1]
    tiles_per_mem = (n // tile) // (mk.shape[0] // groups)
    row = lambda w: pl.BlockSpec((tile, w), lambda i: (i, 0))
    full = lambda arr: pl.BlockSpec(arr.shape, lambda i: (0,) * arr.ndim)
    mspec = pl.BlockSpec((groups, mem_len, d), lambda i: (i // tiles_per_mem, 0, 0))
    return pl.pallas_call(
        functools.partial(_mix_xattn_kernel, groups=groups, rows=rows, xheads=xheads, xhd=d // xheads),
        grid=(n // tile,),
        in_specs=[row(d), row(c), row(a), full(wa), full(wb), full(g2), full(wq), mspec, mspec, full(wo)],
        out_specs=row(d), out_shape=jax.ShapeDtypeStruct((n, d), F32),
        compiler_params=_params("parallel"), name="mix_xattn",
    )(x, yconv, att, wa, wb, g2, wq, mk, mv, wo)


def _extract_topk(s, order, payload, k):
    vals, pays = [], []
    big = jnp.int32(2 ** 30)
    for _ in range(k):
        m = jnp.max(s, axis=0, keepdims=True)
        pos = jnp.min(jnp.where(s == m, order, big), axis=0, keepdims=True)
        hit = order == pos
        vals.append(m)
        pays.append(pos if payload is None else jnp.max(jnp.where(hit, payload, -1), axis=0, keepdims=True))
        s = jnp.where(hit, -jnp.inf, s)
    return jnp.concatenate(vals, axis=0), jnp.concatenate(pays, axis=0)


def _route_kernel(y_ref, g_ref, wq_ref, sk_ref, hn_ref, idx_ref, gate_ref, *, heads, nkeys, half, topk):
    hn = _rms(y_ref[...], g_ref[...])
    hn_ref[...] = hn
    q = _dot(hn.astype(BF16), wq_ref[...])
    t = q.shape[0]
    key_order = lax.broadcasted_iota(I32, (nkeys, t), 0)
    sub = lax.broadcasted_iota(I32, (SUBLANES, t), 0)
    groups = [(0, 0, 8), (0, 8, 8)] + [(a, 0, min(8, topk // (a + 1))) for a in range(1, 8)]
    cand_order = jnp.concatenate([a * topk + b0 + sub for a, b0, _ in groups] + [(sub + 8) * topk], axis=0)
    for h in range(heads):
        top_s, top_i = [], []
        for c in range(2):
            lo = (2 * h + c) * half
            s = _dot_nt(sk_ref[h, c], q[:, lo:lo + half].astype(BF16))
            vs, ids = _extract_topk(s, key_order, None, topk)
            top_s.append(vs)
            top_i.append(ids)
        cs, ci = [], []
        for a, b0, nvalid in groups:
            s_ab = top_s[0][a:a + 1] + top_s[1][b0:b0 + SUBLANES]
            cs.append(s_ab if nvalid == SUBLANES else jnp.where(sub < nvalid, s_ab, -jnp.inf))
            ci.append(top_i[0][a:a + 1] * nkeys + top_i[1][b0:b0 + SUBLANES])
        cs.append(top_s[0][8:topk] + top_s[1][0:1])
        ci.append(top_i[0][8:topk] * nkeys + top_i[1][0:1])
        best_s, best_i = _extract_topk(jnp.concatenate(cs, axis=0), cand_order,
                                       jnp.concatenate(ci, axis=0), topk)
        e = jnp.exp(best_s - jnp.max(best_s, axis=0, keepdims=True))
        gate_ref[h * topk:(h + 1) * topk, :] = e / jnp.sum(e, axis=0, keepdims=True)
        idx_ref[h * topk:(h + 1) * topk, :] = best_i


def _route(y, g, wq, sk, *, topk, tile=256):
    n, d = y.shape
    heads, _, nkeys, half = sk.shape
    assert topk == 2 * SUBLANES, "the candidate grouping in _route_kernel is laid out for top-16"
    tile = min(tile, n)
    row = pl.BlockSpec((tile, d), lambda i: (i, 0))
    col = pl.BlockSpec((heads * topk, tile), lambda i: (0, i))
    full = lambda a: pl.BlockSpec(a.shape, lambda i: (0,) * a.ndim)
    return pl.pallas_call(
        functools.partial(_route_kernel, heads=heads, nkeys=nkeys, half=half, topk=topk),
        grid=(n // tile,), in_specs=[row, full(g), full(wq), full(sk)], out_specs=(row, col, col),
        out_shape=(jax.ShapeDtypeStruct((n, d), F32), jax.ShapeDtypeStruct((heads * topk, n), I32),
                   jax.ShapeDtypeStruct((heads * topk, n), F32)),
        compiler_params=_params("parallel"), name="peer_route",
    )(y, g, wq, sk)


def _experts_kernel(idx_hbm, hn_ref, gate_ref, y_ref, gf_ref, tab_hbm, o_ref,
                    idx_smem, rows, y3, isem, rsem, *, tt, ne, d, depth, group, final_norm):
    i = pl.program_id(0)
    nblk = pl.num_programs(0)
    blk_len = tt * ne
    cur = (i % IDX_SLOTS) * blk_len
    nxt = jnp.where(i + 1 < nblk, (i + 1) % IDX_SLOTS, i % IDX_SLOTS) * blk_len

    def idx_copy(blk):
        sl = blk % IDX_SLOTS
        dst = idx_smem.at[pl.ds(pl.multiple_of(sl * blk_len, blk_len), blk_len)]
        return pltpu.make_async_copy(idx_hbm.at[blk], dst, isem.at[sl])

    nslab = 2 * d // LANES

    def row_copy(src_row, rs, e):
        return pltpu.make_async_copy(tab_hbm.at[src_row], rows.at[rs, pl.ds(e * SLAB_STRIDE, nslab), :],
                                     rsem.at[rs])

    def issue(base, rs):
        ids = idx_smem.at[pl.ds(base, ne)]
        for e in range(ne):
            row_copy(ids[e], rs, e).start(priority=e % 2)

    def wait_rows(rs):
        for e in range(ne):
            row_copy(0, rs, e).wait()

    @pl.when(i == 0)
    def _():
        idx_copy(0).start()

        @pl.when(nblk > 1)
        def _():
            idx_copy(1).start()

        idx_copy(0).wait()
        for t in range(depth):
            issue(t * ne, t)

    @pl.when(i + 1 < nblk)
    def _():
        idx_copy(i + 1).wait()

    @pl.when(i + 2 < nblk)
    def _():
        idx_copy(i + 2).start()

    nct = d // LANES

    token_lane = lax.broadcasted_iota(I32, (ne, tt), 1)

    def compute(t, rs):
        def chunk(c):
            return rows[rs, pl.ds(c, ne, stride=SLAB_STRIDE), :]

        x = hn_ref[pl.ds(t, 1), :]
        hp = chunk(0) * x[:, 0:LANES]
        for c in range(1, nct):
            hp = hp + chunk(c) * x[:, c * LANES:(c + 1) * LANES]
        h = jnp.sum(hp, axis=1, keepdims=True)
        act = 0.5 * h * (1.0 + lax.erf(h * (2.0 ** -0.5)))
        gate = jnp.sum(jnp.where(token_lane == t, gate_ref[0], 0.0), axis=1, keepdims=True)
        w = gate * act
        r = jnp.concatenate([jnp.sum(w * chunk(nct + c), axis=0, keepdims=True) for c in range(nct)], axis=1)
        y3[pl.ds(t, 1), :] = y_ref[pl.ds(t, 1), :] + r

    def body(k, carry):
        for g0 in range(0, depth, group):
            slots = range(g0, g0 + group)
            for rs in slots:
                wait_rows(rs)
            for rs in slots:
                compute(k * depth + rs, rs)
            for rs in slots:
                ahead = (k + 1) * depth + rs
                issue(jnp.where(ahead < tt, cur + ahead * ne, nxt + (ahead - tt) * ne), rs)
        return carry

    lax.fori_loop(0, tt // depth, body, 0)

    @pl.when(i == nblk - 1)
    def _():
        for rs in range(depth):
            wait_rows(rs)

    o_ref[...] = _rms(y3[...], gf_ref[...]) if final_norm else y3[...]


def _experts(idx_t, hn, gate_t, y, gf, table, *, final_norm, tt=64, depth=8, group=4):
    n, d = hn.shape
    ne = idx_t.shape[0]
    tt = min(tt, n)
    assert tt % depth == 0 and depth % group == 0 and n % tt == 0 and 2 * d // LANES <= SLAB_STRIDE
    idx_blocks = idx_t.T.reshape(n // tt, tt * ne)
    gates = gate_t.reshape(ne, n // tt, tt).transpose(1, 0, 2)
    row = lambda w: pl.BlockSpec((tt, w), lambda i: (i, 0))
    anyspec = pl.BlockSpec(memory_space=pl.ANY)
    return pl.pallas_call(
        functools.partial(_experts_kernel, tt=tt, ne=ne, d=d, depth=depth, group=group,
                          final_norm=final_norm),
        grid=(n // tt,),
        in_specs=[anyspec, row(d), pl.BlockSpec((1, ne, tt), lambda i: (i, 0, 0)), row(d),
                  pl.BlockSpec(gf.shape, lambda i: (0, 0)), anyspec],
        out_specs=row(d), out_shape=jax.ShapeDtypeStruct((n, d), F32),
        scratch_shapes=[pltpu.SMEM((IDX_SLOTS * tt * ne,), I32),
                        pltpu.VMEM((depth, ne * SLAB_STRIDE, LANES), F32),
                        pltpu.VMEM((tt, d), F32), pltpu.SemaphoreType.DMA((IDX_SLOTS,)),
                        pltpu.SemaphoreType.DMA((depth,))],
        compiler_params=_params("arbitrary"), name="peer_experts",
    )(idx_blocks, hn, gates, y, gf, table)


def _peer(y, g3, wq_b, sk_b, table, gf, *, topk, final_norm):
    hn, idx_t, gate_t = _route(y, g3, wq_b, sk_b, topk=topk)
    return _experts(idx_t, hn, gate_t, y, gf, table, final_norm=final_norm)


def kernel(x_prompt, x_sample, mem_prompt, cache_k, cache_v, cache_logf, page_table, state_conv, cache_mem_k, cache_mem_v, norm1_g, w_in, b_f, conv_w, conv_b, conv_ln_g, conv_ln_b, w_out, norm2_g, mem_norm_g, wq_x, wk_x, wv_x, wo_x, norm3_g, w_query, sub_keys, expert_u, expert_v, normf_g):
    depth = w_in.shape[0]
    bsz, seq, d = x_prompt.shape
    nb, t, _ = x_sample.shape
    heads, hd = cache_k.shape[3], cache_k.shape[4]
    fw = heads * hd
    cc = conv_w.shape[2]
    xheads = cache_mem_k.shape[3]
    mem_len = mem_prompt.shape[1]
    peer_heads = sub_keys.shape[1]
    topk = 16
    n_pool, page = cache_k.shape[1], cache_k.shape[2]
    row = lambda a: a.reshape(1, -1)

    yp = x_prompt.reshape(bsz * seq, d)
    ys = x_sample.reshape(nb * t, d)
    outs = {k: [] for k in ("kp", "vp", "fp", "cp", "mkp", "mvp", "ks", "vs", "fs", "cs")}
    gf = row(normf_g)
    for l in range(depth):
        last = l == depth - 1
        w_main = w_in[l, :, :2 * cc + 3 * fw].astype(BF16)
        w_f = jnp.pad(w_in[l, :, 2 * cc + 3 * fw:], ((0, 0), (0, LANES - heads))).astype(BF16)
        wa = w_out[l, :cc].astype(BF16)
        wb = w_out[l, cc:].astype(BF16)
        wq_b, wo_b = wq_x[l].astype(BF16), wo_x[l].astype(BF16)
        wpq_b = w_query[l].astype(BF16)
        sk_b = sub_keys[l].astype(BF16)
        table = jnp.concatenate([expert_u[l], expert_v[l]], axis=1).reshape(-1, 2 * d // LANES, LANES)
        cw, cb, lg, lb = conv_w[l], row(conv_b[l]), row(conv_ln_g[l]), row(conv_ln_b[l])
        inproj = functools.partial(_inproj, g=row(norm1_g[l]), w_main=w_main, w_f=w_f, b_f=row(b_f[l]),
                                   cc=cc, fw=fw, heads=heads, head_dim=hd)

        glu, qb, k, v, kb, vb, lf = inproj(ys)
        keys_minor = lambda c: jnp.transpose(c, (0, 2, 3, 1)).reshape(n_pool, fw, page)
        lf_new_t = jnp.pad(lf.reshape(nb, t, heads).transpose(0, 2, 1), ((0, 0), (0, 0), (0, page - t)))
        att = _fox_sample(page_table, qb.astype(F32), k, v, lf_new_t,
                          keys_minor(cache_k[l]), keys_minor(cache_v[l]),
                          jnp.transpose(cache_logf[l], (0, 2, 1)), t=t, heads=heads, hd=hd)
        yconv, new_state = _conv_sample(state_conv[l], glu.reshape(nb, t, cc), cw, cb, lg, lb)
        groups = min(SAMPLE_GROUPS, nb)
        y2 = _mix_xattn(ys, yconv.reshape(nb * t, cc), att, wa, wb, row(norm2_g[l]), wq_b,
                        cache_mem_k[l].reshape(nb, mem_len, d), cache_mem_v[l].reshape(nb, mem_len, d), wo_b,
                        groups=groups, rows=t, xheads=xheads)
        ys = _peer(y2, row(norm3_g[l]), wpq_b, sk_b, table, gf, topk=topk, final_norm=last)
        outs["ks"].append(k.reshape(nb, t, heads, hd))
        outs["vs"].append(v.reshape(nb, t, heads, hd))
        outs["fs"].append(lf.reshape(nb, t, heads))
        outs["cs"].append(new_state)

        glu, qb, k, v, kb, vb, lf = inproj(yp)
        pieces = _neg_cumsum_pieces(lf.reshape(bsz, seq, heads).transpose(0, 2, 1))
        per_head = lambda a: a.reshape(bsz, seq, heads, hd).transpose(0, 2, 1, 3)
        pad = LANES - hd - len(pieces)
        k_aug = jnp.concatenate([per_head(kb), jnp.stack(pieces, axis=-1).astype(BF16),
                                 jnp.zeros((bsz, heads, seq, pad), BF16)], axis=-1)
        q_aug = jnp.concatenate([per_head(qb).transpose(0, 1, 3, 2),
                                 jnp.ones((bsz, heads, len(pieces), seq), BF16),
                                 jnp.zeros((bsz, heads, pad, seq), BF16)], axis=2)
        att_t = _fox_prompt(q_aug, k_aug, per_head(vb).transpose(0, 1, 3, 2))
        att = att_t.transpose(0, 3, 1, 2)
        glu3 = glu.reshape(bsz, seq, cc)
        yconv = _conv_prompt(glu3, cw, cb, lg, lb)
        mk, mv = _memkv(mem_prompt.reshape(bsz * mem_len, d), row(mem_norm_g[l]),
                        wk_x[l].astype(BF16), wv_x[l].astype(BF16))
        y2 = _mix_xattn(yp, yconv.reshape(bsz * seq, cc), att.reshape(bsz * seq, fw), wa, wb, row(norm2_g[l]),
                        wq_b, mk.reshape(bsz, mem_len, d), mv.reshape(bsz, mem_len, d), wo_b,
                        groups=1, rows=min(512, seq), xheads=xheads)
        yp = _peer(y2, row(norm3_g[l]), wpq_b, sk_b, table, gf, topk=topk, final_norm=last)
        outs["kp"].append(k.reshape(bsz, seq, heads, hd))
        outs["vp"].append(v.reshape(bsz, seq, heads, hd))
        outs["fp"].append(lf.reshape(bsz, seq, heads))
        outs["cp"].append(glu3[:, seq - (cw.shape[0] - 1):])
        outs["mkp"].append(mk.reshape(bsz, mem_len, xheads, d // xheads))
        outs["mvp"].append(mv.reshape(bsz, mem_len, xheads, d // xheads))
    st = lambda key: jnp.stack(outs[key])
    return (yp.reshape(bsz, seq, d), ys.reshape(nb, t, d), st("kp"), st("vp"), st("fp"), st("cp"),
            st("mkp"), st("mvp"), st("ks"), st("vs"), st("fs"), st("cs"))
```

```python
import functools

import jax
import jax.numpy as jnp
from jax import lax
from jax.experimental import pallas as pl
from jax.experimental.pallas import tpu as pltpu

F32 = jnp.float32
BF16 = jnp.bfloat16
I32 = jnp.int32

EPS = 1e-6
NEG_INF = -1e30
LANES = 128
SUBLANES = 8
VMEM_LIMIT = 48 * 1024 * 1024
SAMPLE_GROUPS = 4
IDX_SLOTS = 3
SLAB_STRIDE = 20
PAGE_SLOTS = 3


def _params(*sem):
    return pltpu.CompilerParams(dimension_semantics=sem, vmem_limit_bytes=VMEM_LIMIT)


def _dot(a, b):
    return jnp.dot(a, b, preferred_element_type=F32)


def _dot_nt(a, b):
    return lax.dot_general(a, b, (((1,), (1,)), ((), ())), preferred_element_type=F32)


def _rms(x, g):
    return x * lax.rsqrt(jnp.mean(x * x, axis=-1, keepdims=True) + EPS) * g


def _split3(x):
    hi = x.astype(BF16)
    r = x - hi.astype(F32)
    mid = r.astype(BF16)
    lo = (r - mid.astype(F32)).astype(BF16)
    return hi, mid, lo


def _inproj_kernel(x_ref, g_ref, w_ref, wf_ref, bf_ref,
                   glu_ref, q_ref, k_ref, v_ref, kb_ref, vb_ref, lf_ref, *, cc, fw, heads, qscale):
    xb = _rms(x_ref[...], g_ref[...]).astype(BF16)

    def mm(lo, n):
        return _dot(xb, w_ref[:, lo:lo + n])

    glu_ref[...] = mm(0, cc) * jax.nn.sigmoid(mm(cc, cc))
    q_ref[...] = (mm(2 * cc, fw) * qscale).astype(BF16)
    k = mm(2 * cc + fw, fw)
    k_ref[...] = k
    kb_ref[...] = k.astype(BF16)
    v = mm(2 * cc + 2 * fw, fw)
    v_ref[...] = v
    vb_ref[...] = v.astype(BF16)
    z = _dot(xb, wf_ref[...])[:, :heads] + bf_ref[...]
    lf_ref[...] = jnp.minimum(z, 0.0) - jnp.log1p(jnp.exp(-jnp.abs(z)))


def _inproj(x, g, w_main, w_f, b_f, *, cc, fw, heads, head_dim, tile=512):
    n, d = x.shape
    tile = min(tile, n)
    row = lambda w: pl.BlockSpec((tile, w), lambda i: (i, 0))
    full = lambda a: pl.BlockSpec(a.shape, lambda i: (0,) * a.ndim)
    out_shape = (jax.ShapeDtypeStruct((n, cc), F32), jax.ShapeDtypeStruct((n, fw), BF16),
                 jax.ShapeDtypeStruct((n, fw), F32), jax.ShapeDtypeStruct((n, fw), F32),
                 jax.ShapeDtypeStruct((n, fw), BF16), jax.ShapeDtypeStruct((n, fw), BF16),
                 jax.ShapeDtypeStruct((n, heads), F32))
    return pl.pallas_call(
        functools.partial(_inproj_kernel, cc=cc, fw=fw, heads=heads, qscale=head_dim ** -0.5),
        grid=(n // tile,),
        in_specs=[row(d), full(g), full(w_main), full(w_f), full(b_f)],
        out_specs=(row(cc), row(fw), row(fw), row(fw), row(fw), row(fw), row(heads)),
        out_shape=out_shape, compiler_params=_params("parallel"), name="inproj",
    )(x, g, w_main, w_f, b_f)


def _cumsum_kernel(x_ref, hi_ref, mid_ref, lo_ref, *, nchunk):
    h = x_ref.shape[1]
    lane = lax.broadcasted_iota(I32, (h, LANES), 1)

    def body(i, carry):
        off = pl.multiple_of(i * LANES, LANES)
        y = x_ref[0, :, pl.ds(off, LANES)]
        d = 1
        while d < LANES:
            y = y + jnp.where(lane >= d, pltpu.roll(y, d, axis=1), 0.0)
            d *= 2
        y = y + carry
        for ref, piece in zip((hi_ref, mid_ref, lo_ref), _split3(-y)):
            ref[0, :, pl.ds(off, LANES)] = piece.astype(F32)
        return y[:, LANES - 1:LANES]

    lax.fori_loop(0, nchunk, body, jnp.zeros((h, 1), F32))


def _neg_cumsum_pieces(x):
    b, h, s = x.shape
    spec = pl.BlockSpec((1, h, s), lambda i: (i, 0, 0))
    shape = jax.ShapeDtypeStruct(x.shape, F32)
    return pl.pallas_call(
        functools.partial(_cumsum_kernel, nchunk=s // LANES), grid=(b,),
        in_specs=[spec], out_specs=(spec, spec, spec), out_shape=(shape, shape, shape),
        compiler_params=_params("parallel"), name="cumsum",
    )(x)


def _fox_prompt_kernel(q_ref, k_ref, v_ref, o_ref, m_ref, l_ref, acc_ref, s_a, s_b, *, tq):
    i = pl.program_id(2)
    nh = q_ref.shape[1]
    m_ref[...] = jnp.full(m_ref.shape, NEG_INF, F32)
    l_ref[...] = jnp.zeros(l_ref.shape, F32)
    acc_ref[...] = jnp.zeros(acc_ref.shape, F32)

    def logits(j, buf):
        off = pl.multiple_of(j * tq, tq)
        for h in range(nh):
            buf[h] = _dot(k_ref[0, h, pl.ds(off, tq), :], q_ref[0, h])

    def softmax_pv(j, buf, masked):
        off = pl.multiple_of(j * tq, tq)
        ps, alphas = [], []
        for h in range(nh):
            s = buf[h]
            if masked:
                key = lax.broadcasted_iota(I32, (tq, tq), 0)
                qry = lax.broadcasted_iota(I32, (tq, tq), 1)
                s = jnp.where(key <= qry, s, NEG_INF)
            m_old = m_ref[h]
            m_new = jnp.maximum(m_old, jnp.max(s, axis=0, keepdims=True))
            p = jnp.exp(s - m_new)
            alpha = jnp.exp(m_old - m_new)
            l_ref[h] = alpha * l_ref[h] + jnp.sum(p, axis=0, keepdims=True)
            m_ref[h] = m_new
            ps.append(p.astype(BF16))
            alphas.append(alpha)
        pvs = [_dot(v_ref[0, h, :, pl.ds(off, tq)], ps[h]) for h in range(nh)]
        for h in range(nh):
            acc_ref[h] = acc_ref[h] * alphas[h] + pvs[h]

    logits(0, s_a)

    def pair(jj, carry):
        logits(2 * jj + 1, s_b)
        softmax_pv(2 * jj, s_a, False)
        logits(2 * jj + 2, s_a)
        softmax_pv(2 * jj + 1, s_b, False)
        return carry

    lax.fori_loop(0, i // 2, pair, 0)

    @pl.when(i % 2 == 0)
    def _():
        softmax_pv(i, s_a, True)

    @pl.when(i % 2 == 1)
    def _():
        logits(i, s_b)
        softmax_pv(i - 1, s_a, False)
        softmax_pv(i, s_b, True)

    for h in range(nh):
        o_ref[0, h] = (acc_ref[h] / l_ref[h]).astype(o_ref.dtype)


def _fox_prompt(q_aug, k_aug, v_t, *, tq=512, nh=4):
    b, heads, aug, s = q_aug.shape
    hd = v_t.shape[2]
    tq = min(tq, s)
    qspec = pl.BlockSpec((1, nh, aug, tq), lambda bi, p, i: (bi, p, 0, i))
    once = dict(pipeline_mode=pl.Buffered(1))
    kspec = pl.BlockSpec((1, nh, s, aug), lambda bi, p, i: (bi, p, 0, 0), **once)
    vspec = pl.BlockSpec((1, nh, hd, s), lambda bi, p, i: (bi, p, 0, 0), **once)
    ospec = pl.BlockSpec((1, nh, hd, tq), lambda bi, p, i: (bi, p, 0, i))
    return pl.pallas_call(
        functools.partial(_fox_prompt_kernel, tq=tq),
        grid=(b, heads // nh, s // tq),
        in_specs=[qspec, kspec, vspec], out_specs=ospec,
        out_shape=jax.ShapeDtypeStruct((b, heads, hd, s), BF16),
        scratch_shapes=[pltpu.VMEM((nh, 1, tq), F32), pltpu.VMEM((nh, 1, tq), F32),
                        pltpu.VMEM((nh, hd, tq), F32),
                        pltpu.VMEM((nh, tq, tq), F32), pltpu.VMEM((nh, tq, tq), F32)],
        compiler_params=_params("parallel", "parallel", "arbitrary"), name="fox_prompt",
    )(q_aug, k_aug, v_t)


def _fox_sample_kernel(pt_ref, q_ref, kn_ref, vn_ref, lfn_ref, ck_hbm, cv_hbm, clf_hbm, o_ref,
                       kbuf, vbuf, lfbuf, knew, vnew, m_ref, l_ref, acc_ref, pre_ref, sem,
                       *, nb, t, heads, hd, pg, page, ngroups):
    rows = heads * t
    width = heads * hd

    def copies(step, slot):
        b = step // ngroups
        g = step % ngroups
        out = []
        for i in range(pg):
            pid = pt_ref[b, g * pg + i]
            out.append(pltpu.make_async_copy(ck_hbm.at[pid], kbuf.at[slot, i], sem.at[slot, 0]))
            out.append(pltpu.make_async_copy(cv_hbm.at[pid], vbuf.at[slot, i], sem.at[slot, 1]))
            out.append(pltpu.make_async_copy(clf_hbm.at[pid], lfbuf.at[slot, i], sem.at[slot, 2]))
        return out

    knew[...] = jnp.zeros(knew.shape, F32)
    vnew[...] = jnp.zeros(vnew.shape, F32)

    r_i = lax.broadcasted_iota(I32, (rows, width), 0)
    c_i = lax.broadcasted_iota(I32, (rows, width), 1)
    blockdiag = (r_i // t) == (c_i // hd)
    u_r = lax.broadcasted_iota(I32, (page, page), 0)
    u_c = lax.broadcasted_iota(I32, (page, page), 1)
    tri = jnp.where(u_r <= u_c, 1.0, 0.0).astype(BF16)
    n_r = lax.broadcasted_iota(I32, (rows, page), 0)
    n_c = lax.broadcasted_iota(I32, (rows, page), 1)
    new_mask = n_c <= (n_r % t)

    def pages_update(s_raws, lf_ts, mask, pv_fns):
        carry = pre_ref[...]
        ss = []
        for s_raw, lf_t in zip(s_raws, lf_ts):
            lfx = jnp.concatenate([jnp.broadcast_to(lf_t[h:h + 1, :], (t, page)) for h in range(heads)], axis=0)
            pre = carry
            for piece in _split3(lfx):
                pre = pre + _dot(piece, tri)
            carry = pre[:, page - 1:page]
            ss.append(s_raw - pre)
        pre_ref[...] = carry
        s = ss[0] if len(ss) == 1 else jnp.concatenate(ss, axis=1)
        if mask is not None:
            s = jnp.where(mask, s, NEG_INF)
        m_old = m_ref[...]
        m_new = jnp.maximum(m_old, jnp.max(s, axis=-1, keepdims=True))
        pf = jnp.exp(s - m_new)
        alpha = jnp.exp(m_old - m_new)
        l_ref[...] = alpha * l_ref[...] + jnp.sum(pf, axis=-1, keepdims=True)
        m_ref[...] = m_new
        p = pf.astype(BF16)
        pv = pv_fns[0](p[:, 0:page])
        for i in range(1, len(pv_fns)):
            pv = pv + pv_fns[i](p[:, i * page:(i + 1) * page])
        acc_ref[...] = alpha * acc_ref[...] + pv

    for s0 in range(min(PAGE_SLOTS - 1, nb * ngroups)):
        for c in copies(s0, s0):
            c.start()

    def body(step, carry):
        slot = step % PAGE_SLOTS
        b = step // ngroups
        g = step % ngroups
        row0 = pl.multiple_of(b * t, t)
        ahead = step + PAGE_SLOTS - 1

        @pl.when(ahead < nb * ngroups)
        def _():
            for c in copies(ahead, ahead % PAGE_SLOTS):
                c.start()

        @pl.when(g == 0)
        def _():
            m_ref[...] = jnp.full(m_ref.shape, NEG_INF, F32)
            l_ref[...] = jnp.zeros(l_ref.shape, F32)
            acc_ref[...] = jnp.zeros(acc_ref.shape, F32)
            pre_ref[...] = jnp.zeros(pre_ref.shape, F32)

        q8 = q_ref[pl.ds(row0, t), :]
        qbd = jnp.where(blockdiag, jnp.concatenate([q8] * heads, axis=0), 0.0).astype(BF16)

        for c in copies(step, slot):
            c.wait()
        pages_update([_dot(qbd, kbuf[slot, i].astype(BF16)) for i in range(pg)],
                     [lfbuf[slot, i] for i in range(pg)], None,
                     [lambda p, i=i: _dot_nt(p, vbuf[slot, i].astype(BF16)) for i in range(pg)])

        @pl.when(g == ngroups - 1)
        def _():
            knew[0:t, :] = kn_ref[pl.ds(row0, t), :]
            vnew[0:t, :] = vn_ref[pl.ds(row0, t), :]
            v_n = vnew[...].astype(BF16)
            pages_update([_dot_nt(qbd, knew[...].astype(BF16))], [lfn_ref[b]], new_mask,
                         [lambda p: _dot(p, v_n)])
            o = jnp.where(blockdiag, acc_ref[...] / l_ref[...], 0.0)
            out = o[0:t]
            for h in range(1, heads):
                out = out + o[h * t:(h + 1) * t]
            o_ref[pl.ds(row0, t), :] = out

        return carry

    lax.fori_loop(0, nb * ngroups, body, 0)


def _fox_sample(page_table, q, k_new, v_new, lf_new_t, cache_kt, cache_vt, cache_lft, *, t, heads, hd, pg=8):
    nb, n_pages = page_table.shape
    page = cache_kt.shape[2]
    width = heads * hd
    rows = heads * t
    pg = min(pg, n_pages)
    ngroups = n_pages // pg
    vm = lambda a: pl.BlockSpec(a.shape, lambda i, pt: (0,) * a.ndim)
    anyspec = pl.BlockSpec(memory_space=pl.ANY)
    grid_spec = pltpu.PrefetchScalarGridSpec(
        num_scalar_prefetch=1, grid=(1,),
        in_specs=[vm(q), vm(k_new), vm(v_new), vm(lf_new_t), anyspec, anyspec, anyspec],
        out_specs=pl.BlockSpec((nb * t, width), lambda i, pt: (0, 0)),
        scratch_shapes=[pltpu.VMEM((PAGE_SLOTS, pg, width, page), F32),
                        pltpu.VMEM((PAGE_SLOTS, pg, width, page), F32),
                        pltpu.VMEM((PAGE_SLOTS, pg, heads, page), F32),
                        pltpu.VMEM((page, width), F32), pltpu.VMEM((page, width), F32),
                        pltpu.VMEM((rows, 1), F32), pltpu.VMEM((rows, 1), F32),
                        pltpu.VMEM((rows, width), F32), pltpu.VMEM((rows, 1), F32),
                        pltpu.SemaphoreType.DMA((PAGE_SLOTS, 3))])
    return pl.pallas_call(
        functools.partial(_fox_sample_kernel, nb=nb, t=t, heads=heads, hd=hd, pg=pg, page=page,
                          ngroups=ngroups),
        grid_spec=grid_spec, out_shape=jax.ShapeDtypeStruct((nb * t, width), F32),
        compiler_params=_params("arbitrary"), name="fox_sample",
    )(page_table, q, k_new, v_new, lf_new_t, cache_kt, cache_vt, cache_lft)


def _ln_silu(y, conv_b, ln_g, ln_b):
    y = y + conv_b
    yc = y - jnp.mean(y, axis=-1, keepdims=True)
    y = yc * lax.rsqrt(jnp.mean(yc * yc, axis=-1, keepdims=True) + EPS) * ln_g + ln_b
    return y * jax.nn.sigmoid(y)


def _conv_prompt_kernel(cur_ref, halo_ref, w_ref, cb_ref, lg_ref, lb_ref, o_ref, ext, *, tile, halo, taps, chunk):
    i = pl.program_id(1)
    ext[0:halo, :] = jnp.where(i > 0, halo_ref[0], 0.0)
    ext[halo:halo + tile, :] = cur_ref[0]
    base = halo - (taps - 1)
    for r0 in range(0, tile, chunk):
        acc = jnp.zeros((chunk, ext.shape[1]), F32)
        for j in range(taps):
            acc = acc + w_ref[j:j + 1, :] * ext[r0 + base + j:r0 + base + j + chunk, :]
        o_ref[0, r0:r0 + chunk, :] = _ln_silu(acc, cb_ref[...], lg_ref[...], lb_ref[...])


def _conv_prompt(glu, conv_w, conv_b, ln_g, ln_b, *, tile=512, halo=32, chunk=64):
    b, s, c = glu.shape
    taps = conv_w.shape[0]
    tile = min(tile, s)
    per = tile // halo
    cur = pl.BlockSpec((1, tile, c), lambda bi, i: (bi, i, 0))
    hal = pl.BlockSpec((1, halo, c), lambda bi, i: (bi, jnp.maximum(i * per - 1, 0), 0))
    full = lambda a: pl.BlockSpec(a.shape, lambda bi, i: (0,) * a.ndim)
    return pl.pallas_call(
        functools.partial(_conv_prompt_kernel, tile=tile, halo=halo, taps=taps, chunk=chunk),
        grid=(b, s // tile),
        in_specs=[cur, hal, full(conv_w), full(conv_b), full(ln_g), full(ln_b)],
        out_specs=cur, out_shape=jax.ShapeDtypeStruct((b, s, c), F32),
        scratch_shapes=[pltpu.VMEM((halo + tile, c), F32)],
        compiler_params=_params("parallel", "parallel"), name="conv_prompt",
    )(glu, glu, conv_w, conv_b, ln_g, ln_b)


def _conv_sample_kernel(st_ref, glu_ref, w_ref, cb_ref, lg_ref, lb_ref, o_ref, ns_ref, ext, *, bt, t, taps):
    hist = taps - 1
    for b in range(bt):
        ext[0:hist, :] = st_ref[b]
        ext[hist:hist + t, :] = glu_ref[b]
        acc = jnp.zeros((t, ext.shape[1]), F32)
        for j in range(taps):
            acc = acc + w_ref[j:j + 1, :] * ext[j:j + t, :]
        o_ref[b] = _ln_silu(acc, cb_ref[...], lg_ref[...], lb_ref[...])
        ns_ref[b] = ext[t:t + hist, :]


def _conv_sample(state, glu, conv_w, conv_b, ln_g, ln_b, *, bt=8):
    nb, hist, c = state.shape
    t = glu.shape[1]
    taps = conv_w.shape[0]
    bt = min(bt, nb)
    st = pl.BlockSpec((bt, hist, c), lambda i: (i, 0, 0))
    gl = pl.BlockSpec((bt, t, c), lambda i: (i, 0, 0))
    full = lambda a: pl.BlockSpec(a.shape, lambda i: (0,) * a.ndim)
    return pl.pallas_call(
        functools.partial(_conv_sample_kernel, bt=bt, t=t, taps=taps), grid=(nb // bt,),
        in_specs=[st, gl, full(conv_w), full(conv_b), full(ln_g), full(ln_b)],
        out_specs=(gl, st),
        out_shape=(jax.ShapeDtypeStruct((nb, t, c), F32), jax.ShapeDtypeStruct((nb, hist, c), F32)),
        scratch_shapes=[pltpu.VMEM((hist + t + 2, c), F32)],
        compiler_params=_params("parallel"), name="conv_sample",
    )(state, glu, conv_w, conv_b, ln_g, ln_b)


def _memkv_kernel(m_ref, g_ref, wk_ref, wv_ref, k_ref, v_ref):
    mb = _rms(m_ref[...], g_ref[...]).astype(BF16)
    k_ref[...] = _dot(mb, wk_ref[...])
    v_ref[...] = _dot(mb, wv_ref[...])


def _memkv(mem, g, wk, wv, *, tile=256):
    n, d = mem.shape
    tile = min(tile, n)
    row = pl.BlockSpec((tile, d), lambda i: (i, 0))
    full = lambda a: pl.BlockSpec(a.shape, lambda i: (0,) * a.ndim)
    return pl.pallas_call(
        _memkv_kernel, grid=(n // tile,), in_specs=[row, full(g), full(wk), full(wv)],
        out_specs=(row, row),
        out_shape=(jax.ShapeDtypeStruct((n, wk.shape[1]), F32), jax.ShapeDtypeStruct((n, wv.shape[1]), F32)),
        compiler_params=_params("parallel"), name="memkv",
    )(mem, g, wk, wv)


def _mix_xattn_kernel(x_ref, yc_ref, att_ref, wa_ref, wb_ref, g2_ref, wq_ref, mk_ref, mv_ref, wo_ref,
                      o_ref, *, groups, rows, xheads, xhd):
    y1 = x_ref[...] + _dot(yc_ref[...].astype(BF16), wa_ref[...]) + _dot(att_ref[...].astype(BF16), wb_ref[...])
    hb = _rms(y1, g2_ref[...]).astype(BF16)
    q = _dot(hb, wq_ref[...]) * (xhd ** -0.5)
    outs = []
    for g in range(groups):
        heads_out = []
        for h in range(xheads):
            cols = slice(h * xhd, (h + 1) * xhd)
            qg = q[g * rows:(g + 1) * rows, cols].astype(BF16)
            s = _dot_nt(qg, mk_ref[g, :, cols].astype(BF16))
            p = jnp.exp(s - jnp.max(s, axis=-1, keepdims=True))
            p = p / jnp.sum(p, axis=-1, keepdims=True)
            heads_out.append(_dot(p.astype(BF16), mv_ref[g, :, cols].astype(BF16)))
        outs.append(jnp.concatenate(heads_out, axis=-1))
    o = outs[0] if groups == 1 else jnp.concatenate(outs, axis=0)
    o_ref[...] = y1 + _dot(o.astype(BF16), wo_ref[...])


def _mix_xattn(x, yconv, att, wa, wb, g2, wq, mk, mv, wo, *, groups, rows, xheads):
    n, d = x.shape
    tile = groups * rows
    c = yconv.shape[1]
    a = att.shape[1]
    mem_len = mk.shape[1]
    tiles_per_mem = (n // tile) // (mk.shape[0] // groups)
    row = lambda w: pl.BlockSpec((tile, w), lambda i: (i, 0))
    full = lambda arr: pl.BlockSpec(arr.shape, lambda i: (0,) * arr.ndim)
    mspec = pl.BlockSpec((groups, mem_len, d), lambda i: (i // tiles_per_mem, 0, 0))
    return pl.pallas_call(
        functools.partial(_mix_xattn_kernel, groups=groups, rows=rows, xheads=xheads, xhd=d // xheads),
        grid=(n // tile,),
        in_specs=[row(d), row(c), row(a), full(wa), full(wb), full(g2), full(wq), mspec, mspec, full(wo)],
        out_specs=row(d), out_shape=jax.ShapeDtypeStruct((n, d), F32),
        compiler_params=_params("parallel"), name="mix_xattn",
    )(x, yconv, att, wa, wb, g2, wq, mk, mv, wo)


def _extract_topk(s, order, payload, k):
    vals, pays = [], []
    big = jnp.int32(2 ** 30)
    for _ in range(k):
        m = jnp.max(s, axis=0, keepdims=True)
        pos = jnp.min(jnp.where(s == m, order, big), axis=0, keepdims=True)
        hit = order == pos
        vals.append(m)
        pays.append(pos if payload is None else jnp.max(jnp.where(hit, payload, -1), axis=0, keepdims=True))
        s = jnp.where(hit, -jnp.inf, s)
    return jnp.concatenate(vals, axis=0), jnp.concatenate(pays, axis=0)


def _route_kernel(y_ref, g_ref, wq_ref, sk_ref, hn_ref, idx_ref, gate_ref, *, heads, nkeys, half, topk):
    hn = _rms(y_ref[...], g_ref[...])
    hn_ref[...] = hn
    q = _dot(hn.astype(BF16), wq_ref[...])
    t = q.shape[0]
    key_order = lax.broadcasted_iota(I32, (nkeys, t), 0)
    sub = lax.broadcasted_iota(I32, (SUBLANES, t), 0)
    groups = [(0, 0, 8), (0, 8, 8)] + [(a, 0, min(8, topk // (a + 1))) for a in range(1, 8)]
    cand_order = jnp.concatenate([a * topk + b0 + sub for a, b0, _ in groups] + [(sub + 8) * topk], axis=0)
    for h in range(heads):
        top_s, top_i = [], []
        for c in range(2):
            lo = (2 * h + c) * half
            s = _dot_nt(sk_ref[h, c], q[:, lo:lo + half].astype(BF16))
            vs, ids = _extract_topk(s, key_order, None, topk)
            top_s.append(vs)
            top_i.append(ids)
        cs, ci = [], []
        for a, b0, nvalid in groups:
            s_ab = top_s[0][a:a + 1] + top_s[1][b0:b0 + SUBLANES]
            cs.append(s_ab if nvalid == SUBLANES else jnp.where(sub < nvalid, s_ab, -jnp.inf))
            ci.append(top_i[0][a:a + 1] * nkeys + top_i[1][b0:b0 + SUBLANES])
        cs.append(top_s[0][8:topk] + top_s[1][0:1])
        ci.append(top_i[0][8:topk] * nkeys + top_i[1][0:1])
        best_s, best_i = _extract_topk(jnp.concatenate(cs, axis=0), cand_order,
                                       jnp.concatenate(ci, axis=0), topk)
        e = jnp.exp(best_s - jnp.max(best_s, axis=0, keepdims=True))
        gate_ref[h * topk:(h + 1) * topk, :] = e / jnp.sum(e, axis=0, keepdims=True)
        idx_ref[h * topk:(h + 1) * topk, :] = best_i


def _route(y, g, wq, sk, *, topk, tile=256):
    n, d = y.shape
    heads, _, nkeys, half = sk.shape
    assert topk == 2 * SUBLANES, "the candidate grouping in _route_kernel is laid out for top-16"
    tile = min(tile, n)
    row = pl.BlockSpec((tile, d), lambda i: (i, 0))
    col = pl.BlockSpec((heads * topk, tile), lambda i: (0, i))
    full = lambda a: pl.BlockSpec(a.shape, lambda i: (0,) * a.ndim)
    return pl.pallas_call(
        functools.partial(_route_kernel, heads=heads, nkeys=nkeys, half=half, topk=topk),
        grid=(n // tile,), in_specs=[row, full(g), full(wq), full(sk)], out_specs=(row, col, col),
        out_shape=(jax.ShapeDtypeStruct((n, d), F32), jax.ShapeDtypeStruct((heads * topk, n), I32),
                   jax.ShapeDtypeStruct((heads * topk, n), F32)),
        compiler_params=_params("parallel"), name="peer_route",
    )(y, g, wq, sk)


def _pack_kernel(u_ref, v_ref, o_ref, *, te, nct):
    for c in range(nct):
        o_ref[pl.ds(c, te, stride=2 * nct), :] = u_ref[:, c * LANES:(c + 1) * LANES]
        o_ref[pl.ds(nct + c, te, stride=2 * nct), :] = v_ref[:, c * LANES:(c + 1) * LANES]


def _pack_table(u, v, *, te=256):
    e, d = u.shape
    nct = d // LANES
    src = pl.BlockSpec((te, d), lambda i: (i, 0))
    return pl.pallas_call(
        functools.partial(_pack_kernel, te=te, nct=nct), grid=(e // te,), in_specs=[src, src],
        out_specs=pl.BlockSpec((te * 2 * nct, LANES), lambda i: (i, 0)),
        out_shape=jax.ShapeDtypeStruct((e * 2 * nct, LANES), F32),
        compiler_params=_params("parallel"), name="pack_table",
    )(u, v)


def _experts_kernel(idx_hbm, hn_ref, gate_ref, y_ref, gf_ref, tab_hbm, o_ref,
                    idx_smem, rows, y3, isem, rsem, *, tt, ne, d, depth, group, final_norm):
    i = pl.program_id(0)
    nblk = pl.num_programs(0)
    blk_len = tt * ne
    cur = (i % IDX_SLOTS) * blk_len
    nxt = jnp.where(i + 1 < nblk, (i + 1) % IDX_SLOTS, i % IDX_SLOTS) * blk_len

    def idx_copy(blk):
        sl = blk % IDX_SLOTS
        dst = idx_smem.at[pl.ds(pl.multiple_of(sl * blk_len, blk_len), blk_len)]
        return pltpu.make_async_copy(idx_hbm.at[blk], dst, isem.at[sl])

    nslab = 2 * d // LANES

    def row_copy(src_row, rs, e):
        return pltpu.make_async_copy(tab_hbm.at[src_row], rows.at[rs, pl.ds(e * SLAB_STRIDE, nslab), :],
                                     rsem.at[rs])

    def issue(base, rs):
        ids = idx_smem.at[pl.ds(base, ne)]
        for e in range(ne):
            row_copy(ids[e], rs, e).start(priority=e % 2)

    def wait_rows(rs):
        for e in range(ne):
            row_copy(0, rs, e).wait()

    @pl.when(i == 0)
    def _():
        idx_copy(0).start()

        @pl.when(nblk > 1)
        def _():
            idx_copy(1).start()

        idx_copy(0).wait()
        for t in range(depth):
            issue(t * ne, t)

    @pl.when(i + 1 < nblk)
    def _():
        idx_copy(i + 1).wait()

    @pl.when(i + 2 < nblk)
    def _():
        idx_copy(i + 2).start()

    nct = d // LANES

    token_lane = lax.broadcasted_iota(I32, (ne, tt), 1)

    def compute(t, rs):
        def chunk(c):
            return rows[rs, pl.ds(c, ne, stride=SLAB_STRIDE), :]

        x = hn_ref[pl.ds(t, 1), :]
        hp = chunk(0) * x[:, 0:LANES]
        for c in range(1, nct):
            hp = hp + chunk(c) * x[:, c * LANES:(c + 1) * LANES]
        h = jnp.sum(hp, axis=1, keepdims=True)
        act = 0.5 * h * (1.0 + lax.erf(h * (2.0 ** -0.5)))
        gate = jnp.sum(jnp.where(token_lane == t, gate_ref[0], 0.0), axis=1, keepdims=True)
        w = gate * act
        r = jnp.concatenate([jnp.sum(w * chunk(nct + c), axis=0, keepdims=True) for c in range(nct)], axis=1)
        y3[pl.ds(t, 1), :] = y_ref[pl.ds(t, 1), :] + r

    def body(k, carry):
        for g0 in range(0, depth, group):
            slots = range(g0, g0 + group)
            for rs in slots:
                wait_rows(rs)
            for rs in slots:
                compute(k * depth + rs, rs)
            for rs in slots:
                ahead = (k + 1) * depth + rs
                issue(jnp.where(ahead < tt, cur + ahead * ne, nxt + (ahead - tt) * ne), rs)
        return carry

    lax.fori_loop(0, tt // depth, body, 0)

    @pl.when(i == nblk - 1)
    def _():
        for rs in range(depth):
            wait_rows(rs)

    o_ref[...] = _rms(y3[...], gf_ref[...]) if final_norm else y3[...]


def _experts(idx_t, hn, gate_t, y, gf, table, *, final_norm, tt=64, depth=8, group=4):
    n, d = hn.shape
    ne = idx_t.shape[0]
    tt = min(tt, n)
    assert tt % depth == 0 and depth % group == 0 and n % tt == 0 and 2 * d // LANES <= SLAB_STRIDE
    idx_blocks = idx_t.T.reshape(n // tt, tt * ne)
    gates = gate_t.reshape(ne, n // tt, tt).transpose(1, 0, 2)
    row = lambda w: pl.BlockSpec((tt, w), lambda i: (i, 0))
    anyspec = pl.BlockSpec(memory_space=pl.ANY)
    return pl.pallas_call(
        functools.partial(_experts_kernel, tt=tt, ne=ne, d=d, depth=depth, group=group,
                          final_norm=final_norm),
        grid=(n // tt,),
        in_specs=[anyspec, row(d), pl.BlockSpec((1, ne, tt), lambda i: (i, 0, 0)), row(d),
                  pl.BlockSpec(gf.shape, lambda i: (0, 0)), anyspec],
        out_specs=row(d), out_shape=jax.ShapeDtypeStruct((n, d), F32),
        scratch_shapes=[pltpu.SMEM((IDX_SLOTS * tt * ne,), I32),
                        pltpu.VMEM((depth, ne * SLAB_STRIDE, LANES), F32),
                        pltpu.VMEM((tt, d), F32), pltpu.SemaphoreType.DMA((IDX_SLOTS,)),
                        pltpu.SemaphoreType.DMA((depth,))],
        compiler_params=_params("arbitrary"), name="peer_experts",
    )(idx_blocks, hn, gates, y, gf, table)


def _peer(y, g3, wq_b, sk_b, table, gf, *, topk, final_norm):
    hn, idx_t, gate_t = _route(y, g3, wq_b, sk_b, topk=topk)
    return _experts(idx_t, hn, gate_t, y, gf, table, final_norm=final_norm)


def kernel(x_prompt, x_sample, mem_prompt, cache_k, cache_v, cache_logf, page_table, state_conv, cache_mem_k, cache_mem_v, norm1_g, w_in, b_f, conv_w, conv_b, conv_ln_g, conv_ln_b, w_out, norm2_g, mem_norm_g, wq_x, wk_x, wv_x, wo_x, norm3_g, w_query, sub_keys, expert_u, expert_v, normf_g):
    depth = w_in.shape[0]
    bsz, seq, d = x_prompt.shape
    nb, t, _ = x_sample.shape
    heads, hd = cache_k.shape[3], cache_k.shape[4]
    fw = heads * hd
    cc = conv_w.shape[2]
    xheads = cache_mem_k.shape[3]
    mem_len = mem_prompt.shape[1]
    peer_heads = sub_keys.shape[1]
    topk = 16
    n_pool, page = cache_k.shape[1], cache_k.shape[2]
    row = lambda a: a.reshape(1, -1)

    yp = x_prompt.reshape(bsz * seq, d)
    ys = x_sample.reshape(nb * t, d)
    outs = {k: [] for k in ("kp", "vp", "fp", "cp", "mkp", "mvp", "ks", "vs", "fs", "cs")}
    gf = row(normf_g)
    for l in range(depth):
        last = l == depth - 1
        w_main = w_in[l, :, :2 * cc + 3 * fw].astype(BF16)
        w_f = jnp.pad(w_in[l, :, 2 * cc + 3 * fw:], ((0, 0), (0, LANES - heads))).astype(BF16)
        wa = w_out[l, :cc].astype(BF16)
        wb = w_out[l, cc:].astype(BF16)
        wq_b, wo_b = wq_x[l].astype(BF16), wo_x[l].astype(BF16)
        wpq_b = w_query[l].astype(BF16)
        sk_b = sub_keys[l].astype(BF16)
        table = _pack_table(expert_u[l], expert_v[l]).reshape(-1, 2 * d // LANES, LANES)
        cw, cb, lg, lb = conv_w[l], row(conv_b[l]), row(conv_ln_g[l]), row(conv_ln_b[l])
        inproj = functools.partial(_inproj, g=row(norm1_g[l]), w_main=w_main, w_f=w_f, b_f=row(b_f[l]),
                                   cc=cc, fw=fw, heads=heads, head_dim=hd)

        glu, qb, k, v, kb, vb, lf = inproj(ys)
        keys_minor = lambda c: jnp.transpose(c, (0, 2, 3, 1)).reshape(n_pool, fw, page)
        lf_new_t = jnp.pad(lf.reshape(nb, t, heads).transpose(0, 2, 1), ((0, 0), (0, 0), (0, page - t)))
        att = _fox_sample(page_table, qb.astype(F32), k, v, lf_new_t,
                          keys_minor(cache_k[l]), keys_minor(cache_v[l]),
                          jnp.transpose(cache_logf[l], (0, 2, 1)), t=t, heads=heads, hd=hd)
        yconv, new_state = _conv_sample(state_conv[l], glu.reshape(nb, t, cc), cw, cb, lg, lb)
        groups = min(SAMPLE_GROUPS, nb)
        y2 = _mix_xattn(ys, yconv.reshape(nb * t, cc), att, wa, wb, row(norm2_g[l]), wq_b,
                        cache_mem_k[l].reshape(nb, mem_len, d), cache_mem_v[l].reshape(nb, mem_len, d), wo_b,
                        groups=groups, rows=t, xheads=xheads)
        ys = _peer(y2, row(norm3_g[l]), wpq_b, sk_b, table, gf, topk=topk, final_norm=last)
        outs["ks"].append(k.reshape(nb, t, heads, hd))
        outs["vs"].append(v.reshape(nb, t, heads, hd))
        outs["fs"].append(lf.reshape(nb, t, heads))
        outs["cs"].append(new_state)

        glu, qb, k, v, kb, vb, lf = inproj(yp)
        pieces = _neg_cumsum_pieces(lf.reshape(bsz, seq, heads).transpose(0, 2, 1))
        per_head = lambda a: a.reshape(bsz, seq, heads, hd).transpose(0, 2, 1, 3)
        pad = LANES - hd - len(pieces)
        k_aug = jnp.concatenate([per_head(kb), jnp.stack(pieces, axis=-1).astype(BF16),
                                 jnp.zeros((bsz, heads, seq, pad), BF16)], axis=-1)
        q_aug = jnp.concatenate([per_head(qb).transpose(0, 1, 3, 2),
                                 jnp.ones((bsz, heads, len(pieces), seq), BF16),
                                 jnp.zeros((bsz, heads, pad, seq), BF16)], axis=2)
        att_t = _fox_prompt(q_aug, k_aug, per_head(vb).transpose(0, 1, 3, 2))
        att = att_t.transpose(0, 3, 1, 2)
        glu3 = glu.reshape(bsz, seq, cc)
        yconv = _conv_prompt(glu3, cw, cb, lg, lb)
        mk, mv = _memkv(mem_prompt.reshape(bsz * mem_len, d), row(mem_norm_g[l]),
                        wk_x[l].astype(BF16), wv_x[l].astype(BF16))
        y2 = _mix_xattn(yp, yconv.reshape(bsz * seq, cc), att.reshape(bsz * seq, fw), wa, wb, row(norm2_g[l]),
                        wq_b, mk.reshape(bsz, mem_len, d), mv.reshape(bsz, mem_len, d), wo_b,
                        groups=1, rows=min(512, seq), xheads=xheads)
        yp = _peer(y2, row(norm3_g[l]), wpq_b, sk_b, table, gf, topk=topk, final_norm=last)
        outs["kp"].append(k.reshape(bsz, seq, heads, hd))
        outs["vp"].append(v.reshape(bsz, seq, heads, hd))
        outs["fp"].append(lf.reshape(bsz, seq, heads))
        outs["cp"].append(glu3[:, seq - (cw.shape[0] - 1):])
        outs["mkp"].append(mk.reshape(bsz, mem_len, xheads, d // xheads))
        outs["mvp"].append(mv.reshape(bsz, mem_len, xheads, d // xheads))
    st = lambda key: jnp.stack(outs[key])
    return (yp.reshape(bsz, seq, d), ys.reshape(nb, t, d), st("kp"), st("vp"), st("fp"), st("cp"),
            st("mkp"), st("mvp"), st("ks"), st("vs"), st("fs"), st("cs"))
```

```python
import functools

import jax
import jax.numpy as jnp
from jax import lax
from jax.experimental import pallas as pl
from jax.experimental.pallas import tpu as pltpu

F32 = jnp.float32
BF16 = jnp.bfloat16
I32 = jnp.int32

EPS = 1e-6
NEG_INF = -1e30
LANES = 128
SUBLANES = 8
VMEM_LIMIT = 48 * 1024 * 1024
SAMPLE_GROUPS = 4
IDX_SLOTS = 3
SLAB_STRIDE = 20
PAGE_SLOTS = 3


def _params(*sem):
    return pltpu.CompilerParams(dimension_semantics=sem, vmem_limit_bytes=VMEM_LIMIT)


def _dot(a, b):
    return jnp.dot(a, b, preferred_element_type=F32)


def _dot_nt(a, b):
    return lax.dot_general(a, b, (((1,), (1,)), ((), ())), preferred_element_type=F32)


def _rms(x, g):
    return x * lax.rsqrt(jnp.mean(x * x, axis=-1, keepdims=True) + EPS) * g


def _split3(x):
    hi = x.astype(BF16)
    r = x - hi.astype(F32)
    mid = r.astype(BF16)
    lo = (r - mid.astype(F32)).astype(BF16)
    return hi, mid, lo


def _inproj_kernel(x_ref, g_ref, w_ref, wf_ref, bf_ref,
                   glu_ref, q_ref, k_ref, v_ref, kb_ref, vb_ref, lf_ref, *, cc, fw, heads, qscale):
    xb = _rms(x_ref[...], g_ref[...]).astype(BF16)

    def mm(lo, n):
        return _dot(xb, w_ref[:, lo:lo + n])

    glu_ref[...] = mm(0, cc) * jax.nn.sigmoid(mm(cc, cc))
    q_ref[...] = (mm(2 * cc, fw) * qscale).astype(BF16)
    k = mm(2 * cc + fw, fw)
    k_ref[...] = k
    kb_ref[...] = k.astype(BF16)
    v = mm(2 * cc + 2 * fw, fw)
    v_ref[...] = v
    vb_ref[...] = v.astype(BF16)
    z = _dot(xb, wf_ref[...])[:, :heads] + bf_ref[...]
    lf_ref[...] = jnp.minimum(z, 0.0) - jnp.log1p(jnp.exp(-jnp.abs(z)))


def _inproj(x, g, w_main, w_f, b_f, *, cc, fw, heads, head_dim, tile=512):
    n, d = x.shape
    tile = min(tile, n)
    row = lambda w: pl.BlockSpec((tile, w), lambda i: (i, 0))
    full = lambda a: pl.BlockSpec(a.shape, lambda i: (0,) * a.ndim)
    out_shape = (jax.ShapeDtypeStruct((n, cc), F32), jax.ShapeDtypeStruct((n, fw), BF16),
                 jax.ShapeDtypeStruct((n, fw), F32), jax.ShapeDtypeStruct((n, fw), F32),
                 jax.ShapeDtypeStruct((n, fw), BF16), jax.ShapeDtypeStruct((n, fw), BF16),
                 jax.ShapeDtypeStruct((n, heads), F32))
    return pl.pallas_call(
        functools.partial(_inproj_kernel, cc=cc, fw=fw, heads=heads, qscale=head_dim ** -0.5),
        grid=(n // tile,),
        in_specs=[row(d), full(g), full(w_main), full(w_f), full(b_f)],
        out_specs=(row(cc), row(fw), row(fw), row(fw), row(fw), row(fw), row(heads)),
        out_shape=out_shape, compiler_params=_params("parallel"), name="inproj",
    )(x, g, w_main, w_f, b_f)


def _cumsum_kernel(x_ref, hi_ref, mid_ref, lo_ref, *, nchunk):
    h = x_ref.shape[1]
    lane = lax.broadcasted_iota(I32, (h, LANES), 1)

    def body(i, carry):
        off = pl.multiple_of(i * LANES, LANES)
        y = x_ref[0, :, pl.ds(off, LANES)]
        d = 1
        while d < LANES:
            y = y + jnp.where(lane >= d, pltpu.roll(y, d, axis=1), 0.0)
            d *= 2
        y = y + carry
        for ref, piece in zip((hi_ref, mid_ref, lo_ref), _split3(-y)):
            ref[0, :, pl.ds(off, LANES)] = piece.astype(F32)
        return y[:, LANES - 1:LANES]

    lax.fori_loop(0, nchunk, body, jnp.zeros((h, 1), F32))


def _neg_cumsum_pieces(x):
    b, h, s = x.shape
    spec = pl.BlockSpec((1, h, s), lambda i: (i, 0, 0))
    shape = jax.ShapeDtypeStruct(x.shape, F32)
    return pl.pallas_call(
        functools.partial(_cumsum_kernel, nchunk=s // LANES), grid=(b,),
        in_specs=[spec], out_specs=(spec, spec, spec), out_shape=(shape, shape, shape),
        compiler_params=_params("parallel"), name="cumsum",
    )(x)


def _fox_prompt_kernel(q_ref, k_ref, v_ref, o_ref, m_ref, l_ref, acc_ref, s_a, s_b, *, tq):
    i = pl.program_id(2)
    nh = q_ref.shape[1]
    m_ref[...] = jnp.full(m_ref.shape, NEG_INF, F32)
    l_ref[...] = jnp.zeros(l_ref.shape, F32)
    acc_ref[...] = jnp.zeros(acc_ref.shape, F32)

    def logits(j, buf):
        off = pl.multiple_of(j * tq, tq)
        for h in range(nh):
            buf[h] = _dot(k_ref[0, h, pl.ds(off, tq), :], q_ref[0, h])

    def softmax_pv(j, buf, masked):
        off = pl.multiple_of(j * tq, tq)
        ps, alphas = [], []
        for h in range(nh):
            s = buf[h]
            if masked:
                key = lax.broadcasted_iota(I32, (tq, tq), 0)
                qry = lax.broadcasted_iota(I32, (tq, tq), 1)
                s = jnp.where(key <= qry, s, NEG_INF)
            m_old = m_ref[h]
            m_new = jnp.maximum(m_old, jnp.max(s, axis=0, keepdims=True))
            p = jnp.exp(s - m_new)
            alpha = jnp.exp(m_old - m_new)
            l_ref[h] = alpha * l_ref[h] + jnp.sum(p, axis=0, keepdims=True)
            m_ref[h] = m_new
            ps.append(p.astype(BF16))
            alphas.append(alpha)
        pvs = [_dot(v_ref[0, h, :, pl.ds(off, tq)], ps[h]) for h in range(nh)]
        for h in range(nh):
            acc_ref[h] = acc_ref[h] * alphas[h] + pvs[h]

    logits(0, s_a)

    def pair(jj, carry):
        logits(2 * jj + 1, s_b)
        softmax_pv(2 * jj, s_a, False)
        logits(2 * jj + 2, s_a)
        softmax_pv(2 * jj + 1, s_b, False)
        return carry

    lax.fori_loop(0, i // 2, pair, 0)

    @pl.when(i % 2 == 0)
    def _():
        softmax_pv(i, s_a, True)

    @pl.when(i % 2 == 1)
    def _():
        logits(i, s_b)
        softmax_pv(i - 1, s_a, False)
        softmax_pv(i, s_b, True)

    for h in range(nh):
        o_ref[0, h] = (acc_ref[h] / l_ref[h]).astype(o_ref.dtype)


def _fox_prompt(q_aug, k_aug, v_t, *, tq=512, nh=4):
    b, heads, aug, s = q_aug.shape
    hd = v_t.shape[2]
    tq = min(tq, s)
    qspec = pl.BlockSpec((1, nh, aug, tq), lambda bi, p, i: (bi, p, 0, i))
    once = dict(pipeline_mode=pl.Buffered(1))
    kspec = pl.BlockSpec((1, nh, s, aug), lambda bi, p, i: (bi, p, 0, 0), **once)
    vspec = pl.BlockSpec((1, nh, hd, s), lambda bi, p, i: (bi, p, 0, 0), **once)
    ospec = pl.BlockSpec((1, nh, hd, tq), lambda bi, p, i: (bi, p, 0, i))
    return pl.pallas_call(
        functools.partial(_fox_prompt_kernel, tq=tq),
        grid=(b, heads // nh, s // tq),
        in_specs=[qspec, kspec, vspec], out_specs=ospec,
        out_shape=jax.ShapeDtypeStruct((b, heads, hd, s), BF16),
        scratch_shapes=[pltpu.VMEM((nh, 1, tq), F32), pltpu.VMEM((nh, 1, tq), F32),
                        pltpu.VMEM((nh, hd, tq), F32),
                        pltpu.VMEM((nh, tq, tq), F32), pltpu.VMEM((nh, tq, tq), F32)],
        compiler_params=_params("parallel", "parallel", "arbitrary"), name="fox_prompt",
    )(q_aug, k_aug, v_t)


def _fox_sample_kernel(pt_ref, q_ref, kn_ref, vn_ref, lfn_ref, ck_hbm, cv_hbm, clf_hbm, o_ref,
                       kbuf, vbuf, lfbuf, knew, vnew, m_ref, l_ref, acc_ref, pre_ref, sem,
                       *, nb, t, heads, hd, pg, page, ngroups):
    rows = heads * t
    width = heads * hd

    def copies(step, slot):
        b = step // ngroups
        g = step % ngroups
        out = []
        for i in range(pg):
            pid = pt_ref[b, g * pg + i]
            out.append(pltpu.make_async_copy(ck_hbm.at[pid], kbuf.at[slot, i], sem.at[slot, 0]))
            out.append(pltpu.make_async_copy(cv_hbm.at[pid], vbuf.at[slot, i], sem.at[slot, 1]))
            out.append(pltpu.make_async_copy(clf_hbm.at[pid], lfbuf.at[slot, i], sem.at[slot, 2]))
        return out

    knew[...] = jnp.zeros(knew.shape, F32)
    vnew[...] = jnp.zeros(vnew.shape, F32)

    r_i = lax.broadcasted_iota(I32, (rows, width), 0)
    c_i = lax.broadcasted_iota(I32, (rows, width), 1)
    blockdiag = (r_i // t) == (c_i // hd)
    u_r = lax.broadcasted_iota(I32, (page, page), 0)
    u_c = lax.broadcasted_iota(I32, (page, page), 1)
    tri = jnp.where(u_r <= u_c, 1.0, 0.0).astype(BF16)
    n_r = lax.broadcasted_iota(I32, (rows, page), 0)
    n_c = lax.broadcasted_iota(I32, (rows, page), 1)
    new_mask = n_c <= (n_r % t)

    def pages_update(s_raws, lf_ts, mask, pv_fns):
        carry = pre_ref[...]
        ss = []
        for s_raw, lf_t in zip(s_raws, lf_ts):
            lfx = jnp.concatenate([jnp.broadcast_to(lf_t[h:h + 1, :], (t, page)) for h in range(heads)], axis=0)
            pre = carry
            for piece in _split3(lfx):
                pre = pre + _dot(piece, tri)
            carry = pre[:, page - 1:page]
            ss.append(s_raw - pre)
        pre_ref[...] = carry
        s = ss[0] if len(ss) == 1 else jnp.concatenate(ss, axis=1)
        if mask is not None:
            s = jnp.where(mask, s, NEG_INF)
        m_old = m_ref[...]
        m_new = jnp.maximum(m_old, jnp.max(s, axis=-1, keepdims=True))
        pf = jnp.exp(s - m_new)
        alpha = jnp.exp(m_old - m_new)
        l_ref[...] = alpha * l_ref[...] + jnp.sum(pf, axis=-1, keepdims=True)
        m_ref[...] = m_new
        p = pf.astype(BF16)
        pv = pv_fns[0](p[:, 0:page])
        for i in range(1, len(pv_fns)):
            pv = pv + pv_fns[i](p[:, i * page:(i + 1) * page])
        acc_ref[...] = alpha * acc_ref[...] + pv

    for s0 in range(min(PAGE_SLOTS - 1, nb * ngroups)):
        for c in copies(s0, s0):
            c.start()

    def body(step, carry):
        slot = step % PAGE_SLOTS
        b = step // ngroups
        g = step % ngroups
        row0 = pl.multiple_of(b * t, t)
        ahead = step + PAGE_SLOTS - 1

        @pl.when(ahead < nb * ngroups)
        def _():
            for c in copies(ahead, ahead % PAGE_SLOTS):
                c.start()

        @pl.when(g == 0)
        def _():
            m_ref[...] = jnp.full(m_ref.shape, NEG_INF, F32)
            l_ref[...] = jnp.zeros(l_ref.shape, F32)
            acc_ref[...] = jnp.zeros(acc_ref.shape, F32)
            pre_ref[...] = jnp.zeros(pre_ref.shape, F32)

        q8 = q_ref[pl.ds(row0, t), :]
        qbd = jnp.where(blockdiag, jnp.concatenate([q8] * heads, axis=0), 0.0).astype(BF16)

        for c in copies(step, slot):
            c.wait()
        pages_update([_dot(qbd, kbuf[slot, i].astype(BF16)) for i in range(pg)],
                     [lfbuf[slot, i] for i in range(pg)], None,
                     [lambda p, i=i: _dot_nt(p, vbuf[slot, i].astype(BF16)) for i in range(pg)])

        @pl.when(g == ngroups - 1)
        def _():
            knew[0:t, :] = kn_ref[pl.ds(row0, t), :]
            vnew[0:t, :] = vn_ref[pl.ds(row0, t), :]
            v_n = vnew[...].astype(BF16)
            pages_update([_dot_nt(qbd, knew[...].astype(BF16))], [lfn_ref[b]], new_mask,
                         [lambda p: _dot(p, v_n)])
            o = jnp.where(blockdiag, acc_ref[...] / l_ref[...], 0.0)
            out = o[0:t]
            for h in range(1, heads):
                out = out + o[h * t:(h + 1) * t]
            o_ref[pl.ds(row0, t), :] = out

        return carry

    lax.fori_loop(0, nb * ngroups, body, 0)


def _fox_sample(page_table, q, k_new, v_new, lf_new_t, cache_kt, cache_vt, cache_lft, *, t, heads, hd, pg=16):
    nb, n_pages = page_table.shape
    page = cache_kt.shape[2]
    width = heads * hd
    rows = heads * t
    pg = min(pg, n_pages)
    ngroups = n_pages // pg
    vm = lambda a: pl.BlockSpec(a.shape, lambda i, pt: (0,) * a.ndim)
    anyspec = pl.BlockSpec(memory_space=pl.ANY)
    grid_spec = pltpu.PrefetchScalarGridSpec(
        num_scalar_prefetch=1, grid=(1,),
        in_specs=[vm(q), vm(k_new), vm(v_new), vm(lf_new_t), anyspec, anyspec, anyspec],
        out_specs=pl.BlockSpec((nb * t, width), lambda i, pt: (0, 0)),
        scratch_shapes=[pltpu.VMEM((PAGE_SLOTS, pg, width, page), F32),
                        pltpu.VMEM((PAGE_SLOTS, pg, width, page), F32),
                        pltpu.VMEM((PAGE_SLOTS, pg, heads, page), F32),
                        pltpu.VMEM((page, width), F32), pltpu.VMEM((page, width), F32),
                        pltpu.VMEM((rows, 1), F32), pltpu.VMEM((rows, 1), F32),
                        pltpu.VMEM((rows, width), F32), pltpu.VMEM((rows, 1), F32),
                        pltpu.SemaphoreType.DMA((PAGE_SLOTS, 3))])
    return pl.pallas_call(
        functools.partial(_fox_sample_kernel, nb=nb, t=t, heads=heads, hd=hd, pg=pg, page=page,
                          ngroups=ngroups),
        grid_spec=grid_spec, out_shape=jax.ShapeDtypeStruct((nb * t, width), F32),
        compiler_params=_params("arbitrary"), name="fox_sample",
    )(page_table, q, k_new, v_new, lf_new_t, cache_kt, cache_vt, cache_lft)


def _ln_silu(y, conv_b, ln_g, ln_b):
    y = y + conv_b
    yc = y - jnp.mean(y, axis=-1, keepdims=True)
    y = yc * lax.rsqrt(jnp.mean(yc * yc, axis=-1, keepdims=True) + EPS) * ln_g + ln_b
    return y * jax.nn.sigmoid(y)


def _conv_prompt_kernel(cur_ref, halo_ref, w_ref, cb_ref, lg_ref, lb_ref, o_ref, ext, *, tile, halo, taps, chunk):
    i = pl.program_id(1)
    ext[0:halo, :] = jnp.where(i > 0, halo_ref[0], 0.0)
    ext[halo:halo + tile, :] = cur_ref[0]
    base = halo - (taps - 1)
    for r0 in range(0, tile, chunk):
        acc = jnp.zeros((chunk, ext.shape[1]), F32)
        for j in range(taps):
            acc = acc + w_ref[j:j + 1, :] * ext[r0 + base + j:r0 + base + j + chunk, :]
        o_ref[0, r0:r0 + chunk, :] = _ln_silu(acc, cb_ref[...], lg_ref[...], lb_ref[...])


def _conv_prompt(glu, conv_w, conv_b, ln_g, ln_b, *, tile=512, halo=32, chunk=64):
    b, s, c = glu.shape
    taps = conv_w.shape[0]
    tile = min(tile, s)
    per = tile // halo
    cur = pl.BlockSpec((1, tile, c), lambda bi, i: (bi, i, 0))
    hal = pl.BlockSpec((1, halo, c), lambda bi, i: (bi, jnp.maximum(i * per - 1, 0), 0))
    full = lambda a: pl.BlockSpec(a.shape, lambda bi, i: (0,) * a.ndim)
    return pl.pallas_call(
        functools.partial(_conv_prompt_kernel, tile=tile, halo=halo, taps=taps, chunk=chunk),
        grid=(b, s // tile),
        in_specs=[cur, hal, full(conv_w), full(conv_b), full(ln_g), full(ln_b)],
        out_specs=cur, out_shape=jax.ShapeDtypeStruct((b, s, c), F32),
        scratch_shapes=[pltpu.VMEM((halo + tile, c), F32)],
        compiler_params=_params("parallel", "parallel"), name="conv_prompt",
    )(glu, glu, conv_w, conv_b, ln_g, ln_b)


def _conv_sample_kernel(st_ref, glu_ref, w_ref, cb_ref, lg_ref, lb_ref, o_ref, ns_ref, ext, *, bt, t, taps):
    hist = taps - 1
    for b in range(bt):
        ext[0:hist, :] = st_ref[b]
        ext[hist:hist + t, :] = glu_ref[b]
        acc = jnp.zeros((t, ext.shape[1]), F32)
        for j in range(taps):
            acc = acc + w_ref[j:j + 1, :] * ext[j:j + t, :]
        o_ref[b] = _ln_silu(acc, cb_ref[...], lg_ref[...], lb_ref[...])
        ns_ref[b] = ext[t:t + hist, :]


def _conv_sample(state, glu, conv_w, conv_b, ln_g, ln_b, *, bt=8):
    nb, hist, c = state.shape
    t = glu.shape[1]
    taps = conv_w.shape[0]
    bt = min(bt, nb)
    st = pl.BlockSpec((bt, hist, c), lambda i: (i, 0, 0))
    gl = pl.BlockSpec((bt, t, c), lambda i: (i, 0, 0))
    full = lambda a: pl.BlockSpec(a.shape, lambda i: (0,) * a.ndim)
    return pl.pallas_call(
        functools.partial(_conv_sample_kernel, bt=bt, t=t, taps=taps), grid=(nb // bt,),
        in_specs=[st, gl, full(conv_w), full(conv_b), full(ln_g), full(ln_b)],
        out_specs=(gl, st),
        out_shape=(jax.ShapeDtypeStruct((nb, t, c), F32), jax.ShapeDtypeStruct((nb, hist, c), F32)),
        scratch_shapes=[pltpu.VMEM((hist + t + 2, c), F32)],
        compiler_params=_params("parallel"), name="conv_sample",
    )(state, glu, conv_w, conv_b, ln_g, ln_b)


def _memkv_kernel(m_ref, g_ref, wk_ref, wv_ref, k_ref, v_ref):
    mb = _rms(m_ref[...], g_ref[...]).astype(BF16)
    k_ref[...] = _dot(mb, wk_ref[...])
    v_ref[...] = _dot(mb, wv_ref[...])


def _memkv(mem, g, wk, wv, *, tile=256):
    n, d = mem.shape
    tile = min(tile, n)
    row = pl.BlockSpec((tile, d), lambda i: (i, 0))
    full = lambda a: pl.BlockSpec(a.shape, lambda i: (0,) * a.ndim)
    return pl.pallas_call(
        _memkv_kernel, grid=(n // tile,), in_specs=[row, full(g), full(wk), full(wv)],
        out_specs=(row, row),
        out_shape=(jax.ShapeDtypeStruct((n, wk.shape[1]), F32), jax.ShapeDtypeStruct((n, wv.shape[1]), F32)),
        compiler_params=_params("parallel"), name="memkv",
    )(mem, g, wk, wv)


def _mix_xattn_kernel(x_ref, yc_ref, att_ref, wa_ref, wb_ref, g2_ref, wq_ref, mk_ref, mv_ref, wo_ref,
                      o_ref, *, groups, rows, xheads, xhd):
    y1 = x_ref[...] + _dot(yc_ref[...].astype(BF16), wa_ref[...]) + _dot(att_ref[...].astype(BF16), wb_ref[...])
    hb = _rms(y1, g2_ref[...]).astype(BF16)
    q = _dot(hb, wq_ref[...]) * (xhd ** -0.5)
    outs = []
    for g in range(groups):
        heads_out = []
        for h in range(xheads):
            cols = slice(h * xhd, (h + 1) * xhd)
            qg = q[g * rows:(g + 1) * rows, cols].astype(BF16)
            s = _dot_nt(qg, mk_ref[g, :, cols].astype(BF16))
            p = jnp.exp(s - jnp.max(s, axis=-1, keepdims=True))
            p = p / jnp.sum(p, axis=-1, keepdims=True)
            heads_out.append(_dot(p.astype(BF16), mv_ref[g, :, cols].astype(BF16)))
        outs.append(jnp.concatenate(heads_out, axis=-1))
    o = outs[0] if groups == 1 else jnp.concatenate(outs, axis=0)
    o_ref[...] = y1 + _dot(o.astype(BF16), wo_ref[...])


def _mix_xattn(x, yconv, att, wa, wb, g2, wq, mk, mv, wo, *, groups, rows, xheads):
    n, d = x.shape
    tile = groups * rows
    c = yconv.shape[1]
    a = att.shape[1]
    mem_len = mk.shape[1]
    tiles_per_mem = (n // tile) // (mk.shape[0] // groups)
    row = lambda w: pl.BlockSpec((tile, w), lambda i: (i, 0))
    full = lambda arr: pl.BlockSpec(arr.shape, lambda i: (0,) * arr.ndim)
    mspec = pl.BlockSpec((groups, mem_len, d), lambda i: (i // tiles_per_mem, 0, 0))
    return pl.pallas_call(
        functools.partial(_mix_xattn_kernel, groups=groups, rows=rows, xheads=xheads, xhd=d // xheads),
        grid=(n // tile,),
        in_specs=[row(d), row(c), row(a), full(wa), full(wb), full(g2), full(wq), mspec, mspec, full(wo)],
        out_specs=row(d), out_shape=jax.ShapeDtypeStruct((n, d), F32),
        compiler_params=_params("parallel"), name="mix_xattn",
    )(x, yconv, att, wa, wb, g2, wq, mk, mv, wo)


def _extract_topk(s, key, k):
    vals, keys = [], []
    big = jnp.int32(2 ** 30)
    for _ in range(k):
        m = jnp.max(s, axis=0, keepdims=True)
        pos = jnp.min(jnp.where(s == m, key, big), axis=0, keepdims=True)
        vals.append(m)
        keys.append(pos)
        s = jnp.where(key == pos, -jnp.inf, s)
    return jnp.concatenate(vals, axis=0), jnp.concatenate(keys, axis=0)


def _route_kernel(y_ref, g_ref, wq_ref, sk_ref, hn_ref, idx_ref, gate_ref, *, heads, nkeys, half, topk):
    hn = _rms(y_ref[...], g_ref[...])
    hn_ref[...] = hn
    q = _dot(hn.astype(BF16), wq_ref[...])
    t = q.shape[0]
    key_order = lax.broadcasted_iota(I32, (nkeys, t), 0)
    sub = lax.broadcasted_iota(I32, (SUBLANES, t), 0)
    groups = [(0, 0, 8), (0, 8, 8)] + [(a, 0, min(8, topk // (a + 1))) for a in range(1, 8)]
    cand_order = jnp.concatenate([a * topk + b0 + sub for a, b0, _ in groups] + [(sub + 8) * topk], axis=0)
    for h in range(heads):
        top_s, top_i = [], []
        for c in range(2):
            lo = (2 * h + c) * half
            s = _dot_nt(sk_ref[h, c], q[:, lo:lo + half].astype(BF16))
            vs, ids = _extract_topk(s, key_order, topk)
            top_s.append(vs)
            top_i.append(ids)
        cs, ci = [], []
        for a, b0, nvalid in groups:
            s_ab = top_s[0][a:a + 1] + top_s[1][b0:b0 + SUBLANES]
            cs.append(s_ab if nvalid == SUBLANES else jnp.where(sub < nvalid, s_ab, -jnp.inf))
            ci.append(top_i[0][a:a + 1] * nkeys + top_i[1][b0:b0 + SUBLANES])
        cs.append(top_s[0][8:topk] + top_s[1][0:1])
        ci.append(top_i[0][8:topk] * nkeys + top_i[1][0:1])
        n_experts = nkeys * nkeys
        best_s, best_key = _extract_topk(jnp.concatenate(cs, axis=0),
                                         cand_order * n_experts + jnp.concatenate(ci, axis=0), topk)
        best_i = best_key & (n_experts - 1)
        e = jnp.exp(best_s - jnp.max(best_s, axis=0, keepdims=True))
        gate_ref[h * topk:(h + 1) * topk, :] = e / jnp.sum(e, axis=0, keepdims=True)
        idx_ref[h * topk:(h + 1) * topk, :] = best_i


def _route(y, g, wq, sk, *, topk, tile=256):
    n, d = y.shape
    heads, _, nkeys, half = sk.shape
    assert topk == 2 * SUBLANES, "the candidate grouping in _route_kernel is laid out for top-16"
    assert nkeys & (nkeys - 1) == 0 and topk * topk * nkeys * nkeys < 2 ** 30, "candidate key packing"
    tile = min(tile, n)
    row = pl.BlockSpec((tile, d), lambda i: (i, 0))
    col = pl.BlockSpec((heads * topk, tile), lambda i: (0, i))
    full = lambda a: pl.BlockSpec(a.shape, lambda i: (0,) * a.ndim)
    return pl.pallas_call(
        functools.partial(_route_kernel, heads=heads, nkeys=nkeys, half=half, topk=topk),
        grid=(n // tile,), in_specs=[row, full(g), full(wq), full(sk)], out_specs=(row, col, col),
        out_shape=(jax.ShapeDtypeStruct((n, d), F32), jax.ShapeDtypeStruct((heads * topk, n), I32),
                   jax.ShapeDtypeStruct((heads * topk, n), F32)),
        compiler_params=_params("parallel"), name="peer_route",
    )(y, g, wq, sk)


def _pack_kernel(u_ref, v_ref, o_ref, *, te, nct):
    for c in range(nct):
        o_ref[pl.ds(c, te, stride=2 * nct), :] = u_ref[:, c * LANES:(c + 1) * LANES]
        o_ref[pl.ds(nct + c, te, stride=2 * nct), :] = v_ref[:, c * LANES:(c + 1) * LANES]


def _pack_table(u, v, *, te=256):
    e, d = u.shape
    nct = d // LANES
    src = pl.BlockSpec((te, d), lambda i: (i, 0))
    return pl.pallas_call(
        functools.partial(_pack_kernel, te=te, nct=nct), grid=(e // te,), in_specs=[src, src],
        out_specs=pl.BlockSpec((te * 2 * nct, LANES), lambda i: (i, 0)),
        out_shape=jax.ShapeDtypeStruct((e * 2 * nct, LANES), F32),
        compiler_params=_params("parallel"), name="pack_table",
    )(u, v)


def _experts_kernel(idx_hbm, hn_ref, gate_ref, y_ref, gf_ref, tab_hbm, o_ref,
                    idx_smem, rows, y3, isem, rsem, *, tt, ne, d, depth, group, final_norm):
    i = pl.program_id(0)
    nblk = pl.num_programs(0)
    blk_len = tt * ne
    cur = (i % IDX_SLOTS) * blk_len
    nxt = jnp.where(i + 1 < nblk, (i + 1) % IDX_SLOTS, i % IDX_SLOTS) * blk_len

    def idx_copy(blk):
        sl = blk % IDX_SLOTS
        dst = idx_smem.at[pl.ds(pl.multiple_of(sl * blk_len, blk_len), blk_len)]
        return pltpu.make_async_copy(idx_hbm.at[blk], dst, isem.at[sl])

    nslab = 2 * d // LANES

    def row_copy(src_row, rs, e):
        return pltpu.make_async_copy(tab_hbm.at[src_row], rows.at[rs, pl.ds(e * SLAB_STRIDE, nslab), :],
                                     rsem.at[rs])

    def issue(base, rs):
        ids = idx_smem.at[pl.ds(base, ne)]
        for e in range(ne):
            row_copy(ids[e], rs, e).start(priority=e % 2)

    def wait_rows(rs):
        for e in range(ne):
            row_copy(0, rs, e).wait()

    @pl.when(i == 0)
    def _():
        idx_copy(0).start()

        @pl.when(nblk > 1)
        def _():
            idx_copy(1).start()

        idx_copy(0).wait()
        for t in range(depth):
            issue(t * ne, t)

    @pl.when(i + 1 < nblk)
    def _():
        idx_copy(i + 1).wait()

    @pl.when(i + 2 < nblk)
    def _():
        idx_copy(i + 2).start()

    nct = d // LANES

    token_lane = lax.broadcasted_iota(I32, (ne, tt), 1)

    def compute(t, rs):
        def chunk(c):
            return rows[rs, pl.ds(c, ne, stride=SLAB_STRIDE), :]

        x = hn_ref[pl.ds(t, 1), :]
        hp = chunk(0) * x[:, 0:LANES]
        for c in range(1, nct):
            hp = hp + chunk(c) * x[:, c * LANES:(c + 1) * LANES]
        h = jnp.sum(hp, axis=1, keepdims=True)
        act = 0.5 * h * (1.0 + lax.erf(h * (2.0 ** -0.5)))
        gate = jnp.sum(jnp.where(token_lane == t, gate_ref[0], 0.0), axis=1, keepdims=True)
        w = gate * act
        r = jnp.concatenate([jnp.sum(w * chunk(nct + c), axis=0, keepdims=True) for c in range(nct)], axis=1)
        y3[pl.ds(t, 1), :] = y_ref[pl.ds(t, 1), :] + r

    def body(k, carry):
        for g0 in range(0, depth, group):
            slots = range(g0, g0 + group)
            for rs in slots:
                wait_rows(rs)
            for rs in slots:
                compute(k * depth + rs, rs)
            for rs in slots:
                ahead = (k + 1) * depth + rs
                issue(jnp.where(ahead < tt, cur + ahead * ne, nxt + (ahead - tt) * ne), rs)
        return carry

    lax.fori_loop(0, tt // depth, body, 0)

    @pl.when(i == nblk - 1)
    def _():
        for rs in range(depth):
            wait_rows(rs)

    o_ref[...] = _rms(y3[...], gf_ref[...]) if final_norm else y3[...]


def _experts(idx_t, hn, gate_t, y, gf, table, *, final_norm, tt=64, depth=8, group=4):
    n, d = hn.shape
    ne = idx_t.shape[0]
    tt = min(tt, n)
    assert tt % depth == 0 and depth % group == 0 and n % tt == 0 and 2 * d // LANES <= SLAB_STRIDE
    idx_blocks = idx_t.T.reshape(n // tt, tt * ne)
    gates = gate_t.reshape(ne, n // tt, tt).transpose(1, 0, 2)
    row = lambda w: pl.BlockSpec((tt, w), lambda i: (i, 0))
    anyspec = pl.BlockSpec(memory_space=pl.ANY)
    return pl.pallas_call(
        functools.partial(_experts_kernel, tt=tt, ne=ne, d=d, depth=depth, group=group,
                          final_norm=final_norm),
        grid=(n // tt,),
        in_specs=[anyspec, row(d), pl.BlockSpec((1, ne, tt), lambda i: (i, 0, 0)), row(d),
                  pl.BlockSpec(gf.shape, lambda i: (0, 0)), anyspec],
        out_specs=row(d), out_shape=jax.ShapeDtypeStruct((n, d), F32),
        scratch_shapes=[pltpu.SMEM((IDX_SLOTS * tt * ne,), I32),
                        pltpu.VMEM((depth, ne * SLAB_STRIDE, LANES), F32),
                        pltpu.VMEM((tt, d), F32), pltpu.SemaphoreType.DMA((IDX_SLOTS,)),
                        pltpu.SemaphoreType.DMA((depth,))],
        compiler_params=_params("arbitrary"), name="peer_experts",
    )(idx_blocks, hn, gates, y, gf, table)


def _peer(y, g3, wq_b, sk_b, table, gf, *, topk, final_norm):
    hn, idx_t, gate_t = _route(y, g3, wq_b, sk_b, topk=topk)
    return _experts(idx_t, hn, gate_t, y, gf, table, final_norm=final_norm)


def kernel(x_prompt, x_sample, mem_prompt, cache_k, cache_v, cache_logf, page_table, state_conv, cache_mem_k, cache_mem_v, norm1_g, w_in, b_f, conv_w, conv_b, conv_ln_g, conv_ln_b, w_out, norm2_g, mem_norm_g, wq_x, wk_x, wv_x, wo_x, norm3_g, w_query, sub_keys, expert_u, expert_v, normf_g):
    depth = w_in.shape[0]
    bsz, seq, d = x_prompt.shape
    nb, t, _ = x_sample.shape
    heads, hd = cache_k.shape[3], cache_k.shape[4]
    fw = heads * hd
    cc = conv_w.shape[2]
    xheads = cache_mem_k.shape[3]
    mem_len = mem_prompt.shape[1]
    peer_heads = sub_keys.shape[1]
    topk = 16
    n_pool, page = cache_k.shape[1], cache_k.shape[2]
    row = lambda a: a.reshape(1, -1)

    yp = x_prompt.reshape(bsz * seq, d)
    ys = x_sample.reshape(nb * t, d)
    outs = {k: [] for k in ("kp", "vp", "fp", "cp", "mkp", "mvp", "ks", "vs", "fs", "cs")}
    gf = row(normf_g)
    for l in range(depth):
        last = l == depth - 1
        w_main = w_in[l, :, :2 * cc + 3 * fw].astype(BF16)
        w_f = jnp.pad(w_in[l, :, 2 * cc + 3 * fw:], ((0, 0), (0, LANES - heads))).astype(BF16)
        wa = w_out[l, :cc].astype(BF16)
        wb = w_out[l, cc:].astype(BF16)
        wq_b, wo_b = wq_x[l].astype(BF16), wo_x[l].astype(BF16)
        wpq_b = w_query[l].astype(BF16)
        sk_b = sub_keys[l].astype(BF16)
        table = _pack_table(expert_u[l], expert_v[l]).reshape(-1, 2 * d // LANES, LANES)
        cw, cb, lg, lb = conv_w[l], row(conv_b[l]), row(conv_ln_g[l]), row(conv_ln_b[l])
        inproj = functools.partial(_inproj, g=row(norm1_g[l]), w_main=w_main, w_f=w_f, b_f=row(b_f[l]),
                                   cc=cc, fw=fw, heads=heads, head_dim=hd)

        glu, qb, k, v, kb, vb, lf = inproj(ys)
        keys_minor = lambda c: jnp.transpose(c, (0, 2, 3, 1)).reshape(n_pool, fw, page)
        lf_new_t = jnp.pad(lf.reshape(nb, t, heads).transpose(0, 2, 1), ((0, 0), (0, 0), (0, page - t)))
        att = _fox_sample(page_table, qb.astype(F32), k, v, lf_new_t,
                          keys_minor(cache_k[l]), keys_minor(cache_v[l]),
                          jnp.transpose(cache_logf[l], (0, 2, 1)), t=t, heads=heads, hd=hd)
        yconv, new_state = _conv_sample(state_conv[l], glu.reshape(nb, t, cc), cw, cb, lg, lb)
        groups = min(SAMPLE_GROUPS, nb)
        y2 = _mix_xattn(ys, yconv.reshape(nb * t, cc), att, wa, wb, row(norm2_g[l]), wq_b,
                        cache_mem_k[l].reshape(nb, mem_len, d), cache_mem_v[l].reshape(nb, mem_len, d), wo_b,
                        groups=groups, rows=t, xheads=xheads)
        ys = _peer(y2, row(norm3_g[l]), wpq_b, sk_b, table, gf, topk=topk, final_norm=last)
        outs["ks"].append(k.reshape(nb, t, heads, hd))
        outs["vs"].append(v.reshape(nb, t, heads, hd))
        outs["fs"].append(lf.reshape(nb, t, heads))
        outs["cs"].append(new_state)

        glu, qb, k, v, kb, vb, lf = inproj(yp)
        pieces = _neg_cumsum_pieces(lf.reshape(bsz, seq, heads).transpose(0, 2, 1))
        per_head = lambda a: a.reshape(bsz, seq, heads, hd).transpose(0, 2, 1, 3)
        pad = LANES - hd - len(pieces)
        k_aug = jnp.concatenate([per_head(kb), jnp.stack(pieces, axis=-1).astype(BF16),
                                 jnp.zeros((bsz, heads, seq, pad), BF16)], axis=-1)
        q_aug = jnp.concatenate([per_head(qb).transpose(0, 1, 3, 2),
                                 jnp.ones((bsz, heads, len(pieces), seq), BF16),
                                 jnp.zeros((bsz, heads, pad, seq), BF16)], axis=2)
        att_t = _fox_prompt(q_aug, k_aug, per_head(vb).transpose(0, 1, 3, 2))
        att = att_t.transpose(0, 3, 1, 2)
        glu3 = glu.reshape(bsz, seq, cc)
        yconv = _conv_prompt(glu3, cw, cb, lg, lb)
        mk, mv = _memkv(mem_prompt.reshape(bsz * mem_len, d), row(mem_norm_g[l]),
                        wk_x[l].astype(BF16), wv_x[l].astype(BF16))
        y2 = _mix_xattn(yp, yconv.reshape(bsz * seq, cc), att.reshape(bsz * seq, fw), wa, wb, row(norm2_g[l]),
                        wq_b, mk.reshape(bsz, mem_len, d), mv.reshape(bsz, mem_len, d), wo_b,
                        groups=1, rows=min(512, seq), xheads=xheads)
        yp = _peer(y2, row(norm3_g[l]), wpq_b, sk_b, table, gf, topk=topk, final_norm=last)
        outs["kp"].append(k.reshape(bsz, seq, heads, hd))
        outs["vp"].append(v.reshape(bsz, seq, heads, hd))
        outs["fp"].append(lf.reshape(bsz, seq, heads))
        outs["cp"].append(glu3[:, seq - (cw.shape[0] - 1):])
        outs["mkp"].append(mk.reshape(bsz, mem_len, xheads, d // xheads))
        outs["mvp"].append(mv.reshape(bsz, mem_len, xheads, d // xheads))
    st = lambda key: jnp.stack(outs[key])
    return (yp.reshape(bsz, seq, d), ys.reshape(nb, t, d), st("kp"), st("vp"), st("fp"), st("cp"),
            st("mkp"), st("mvp"), st("ks"), st("vs"), st("fs"), st("cs"))
```
